```python
import jax, jax.numpy as jnp
from jax import lax
import numpy as np

D_MODEL = 1024
BATCH = 4
SEQ = 8192
DEPTH = 2

HEAD_DIM = 64
H_GDN = (3 * D_MODEL) // (8 * HEAD_DIM)
H_MOBA = D_MODEL // (4 * HEAD_DIM)
H_MLSTM = D_MODEL // HEAD_DIM - H_GDN - H_MOBA
W_GDN = H_GDN * HEAD_DIM
W_MOBA = H_MOBA * HEAD_DIM
W_MLSTM = H_MLSTM * HEAD_DIM
MIX_WIDTH = W_GDN + W_MOBA + W_MLSTM
CONV_WIDTH = 4
GDN_CHUNK = 64
MLSTM_CHUNK = 64
MOBA_BLOCK = 256
MOBA_TOPK = 3
MOBA_Q_BLOCK = 128
D_FF = 2816
N_SUB = 3
ALPHA = (2 * DEPTH) ** 0.25
BETA_INIT = (8 * DEPTH) ** -0.25
ADA_INIT = 0.1
LN_EPS = 1e-5
NORM_EPS = 1e-6

IN_SIZES = [3 * W_GDN, H_GDN, H_GDN, W_GDN,
            3 * W_MOBA,
            2 * W_MLSTM, W_MLSTM, H_MLSTM, H_MLSTM, W_MLSTM]
IN_SPLITS = [int(s) for s in np.cumsum(IN_SIZES)[:-1]]
D_IN = int(sum(IN_SIZES))

kernel_name = "hymba_gdn_moba_mlstm_macaron_deepnorm"


def layer_norm(x, g, b):
    xf = x.astype(jnp.float32)
    mu = jnp.mean(xf, -1, keepdims=True)
    var = jnp.mean(jnp.square(xf - mu), -1, keepdims=True)
    return ((xf - mu) * lax.rsqrt(var + LN_EPS) * g + b).astype(x.dtype)


def modulate(x, shift, scale):
    return x * (1 + scale[:, None, :]) + shift[:, None, :]


def swiglu(h, w13, w2):
    a, b = jnp.split(h @ w13, 2, axis=-1)
    return (jax.nn.silu(a) * b) @ w2


def causal_conv_silu(x, w):
    y = lax.conv_general_dilated(x, w[:, None, :], window_strides=(1,), padding=[(CONV_WIDTH - 1, 0)],
                                 dimension_numbers=('NWC', 'WIO', 'NWC'), feature_group_count=x.shape[-1])
    return jax.nn.silu(y)


def split_heads(t):
    b_, t_, w = t.shape
    return t.reshape(b_, t_, w // HEAD_DIM, HEAD_DIM).transpose(0, 2, 1, 3)


def l2norm(x):
    return x * lax.rsqrt(jnp.sum(x * x, -1, keepdims=True) + NORM_EPS)


def to_chunks(a, size):
    b_, h_, t_ = a.shape[:3]
    return jnp.moveaxis(a.reshape(b_, h_, t_ // size, size, *a.shape[3:]), 2, 0)


def from_chunks(a):
    n, b_, h_, l, d = a.shape
    return jnp.moveaxis(a, 0, 2).reshape(b_, h_, n * l, d)


def gated_delta_rule(q, k, v, g, beta):
    b_, h_, t_, dk = q.shape
    dv = v.shape[-1]
    L = GDN_CHUNK
    q = to_chunks(q * dk ** -0.5, L); k = to_chunks(k, L); v = to_chunks(v, L)
    g = to_chunks(g, L); beta = to_chunks(beta, L)
    G = jnp.cumsum(g, axis=-1)
    incl = jnp.tril(jnp.ones((L, L), dtype=bool))
    strict = jnp.tril(jnp.ones((L, L), dtype=bool), -1)
    decay = jnp.exp(jnp.where(incl, G[..., :, None] - G[..., None, :], -jnp.inf))
    kb = k * beta[..., None]
    A = jnp.where(strict, jnp.einsum('nbhid,nbhjd->nbhij', kb, k) * decay, 0.0)
    M = A + jnp.eye(L, dtype=A.dtype)
    u = lax.linalg.triangular_solve(M, v * beta[..., None], left_side=True, lower=True, unit_diagonal=True)
    w = lax.linalg.triangular_solve(M, kb * jnp.exp(G)[..., None], left_side=True, lower=True, unit_diagonal=True)
    attn = jnp.einsum('nbhid,nbhjd->nbhij', q, k) * decay
    q_dec = q * jnp.exp(G)[..., None]
    k_dec = k * jnp.exp(G[..., -1:] - G)[..., None]
    g_last = jnp.exp(G[..., -1])

    def step(S, inp):
        q_i, w_i, u_i, k_i, a_i, gl_i = inp
        v_new = u_i - jnp.einsum('bhld,bhde->bhle', w_i, S)
        o = jnp.einsum('bhld,bhde->bhle', q_i, S) + jnp.einsum('bhls,bhse->bhle', a_i, v_new)
        S = S * gl_i[..., None, None] + jnp.einsum('bhld,bhle->bhde', k_i, v_new)
        return S, o

    S0 = jnp.zeros((b_, h_, dk, dv), q.dtype)
    _, o = lax.scan(step, S0, (q_dec, w, u, k_dec, attn, g_last))
    return from_chunks(o)


def moba_attention(q, k, v):
    b_, h_, t_, d = q.shape
    n_blk = -(-t_ // MOBA_BLOCK)
    top_k = min(MOBA_TOPK, n_blk)
    pad = n_blk * MOBA_BLOCK - t_
    kp = jnp.pad(k, ((0, 0), (0, 0), (0, pad), (0, 0)))
    vp = jnp.pad(v, ((0, 0), (0, 0), (0, pad), (0, 0)))
    k_blk = kp.reshape(b_, h_, n_blk, MOBA_BLOCK, d)
    v_blk = vp.reshape(b_, h_, n_blk, MOBA_BLOCK, d)
    k_mean = jnp.mean(k_blk, axis=3)
    scale = d ** -0.5
    bi = jnp.arange(b_)[:, None, None, None]
    hi = jnp.arange(h_)[None, :, None, None]
    blk_ids = jnp.arange(n_blk)
    n_sel = top_k * MOBA_BLOCK

    def one_block(qb):
        q0 = qb * MOBA_Q_BLOCK
        own = q0 // MOBA_BLOCK
        qc = lax.dynamic_slice_in_dim(q, q0, MOBA_Q_BLOCK, axis=2)
        gate = jnp.einsum('bhqd,bhnd->bhqn', qc, k_mean)
        gate = jnp.where(blk_ids < own, gate, -jnp.inf)
        _, sel = lax.top_k(gate, top_k)
        sel_ok = jnp.repeat(jnp.arange(top_k) < own, MOBA_BLOCK)
        k_sel = k_blk[bi, hi, sel].reshape(b_, h_, MOBA_Q_BLOCK, n_sel, d)
        v_sel = v_blk[bi, hi, sel].reshape(b_, h_, MOBA_Q_BLOCK, n_sel, d)
        s_sel = jnp.where(sel_ok, jnp.einsum('bhqd,bhqkd->bhqk', qc, k_sel) * scale, -jnp.inf)
        k_own = lax.dynamic_slice_in_dim(kp, own * MOBA_BLOCK, MOBA_BLOCK, axis=2)
        v_own = lax.dynamic_slice_in_dim(vp, own * MOBA_BLOCK, MOBA_BLOCK, axis=2)
        q_pos = q0 + jnp.arange(MOBA_Q_BLOCK)
        k_pos = own * MOBA_BLOCK + jnp.arange(MOBA_BLOCK)
        s_own = jnp.where(k_pos[None, :] <= q_pos[:, None],
                          jnp.einsum('bhqd,bhkd->bhqk', qc, k_own) * scale, -jnp.inf)
        p = jax.nn.softmax(jnp.concatenate([s_sel, s_own], -1).astype(jnp.float32), axis=-1).astype(q.dtype)
        return (jnp.einsum('bhqk,bhqkd->bhqd', p[..., :n_sel], v_sel)
                + jnp.einsum('bhqk,bhkd->bhqd', p[..., n_sel:], v_own))

    out = lax.map(one_block, jnp.arange(t_ // MOBA_Q_BLOCK))
    return from_chunks(out)


def mlstm_chunkwise(q, k, v, i_pre, logf):
    b_, h_, t_, d = q.shape
    L = MLSTM_CHUNK
    q = to_chunks(q, L); k = to_chunks(k * d ** -0.5, L); v = to_chunks(v, L)
    i_pre = to_chunks(i_pre, L); logf = to_chunks(logf, L)
    bcum = jnp.cumsum(logf, axis=-1)
    incl = jnp.tril(jnp.ones((L, L), dtype=bool))
    D = jnp.where(incl, bcum[..., :, None] - bcum[..., None, :] + i_pre[..., None, :], -jnp.inf)
    D_max = jnp.max(D, axis=-1)
    w_end = bcum[..., -1:] - bcum + i_pre
    qk = jnp.einsum('nbhtd,nbhsd->nbhts', q, k)

    def step(carry, inp):
        C, n, m = carry
        q_i, k_i, v_i, b_i, D_i, Dm_i, qk_i, we_i = inp
        inter = b_i + m[..., None]
        m_t = jnp.maximum(inter, Dm_i)
        a_inter = jnp.exp(inter - m_t)
        P = qk_i * jnp.exp(D_i - m_t[..., None])
        num = a_inter[..., None] * jnp.einsum('bhld,bhde->bhle', q_i, C) + jnp.einsum('bhls,bhse->bhle', P, v_i)
        den = a_inter * jnp.einsum('bhld,bhd->bhl', q_i, n) + jnp.sum(P, -1)
        h = num / jnp.maximum(jnp.abs(den), jnp.exp(-m_t))[..., None]
        inter_end = b_i[..., -1] + m
        m_new = jnp.maximum(inter_end, jnp.max(we_i, -1))
        s = jnp.exp(we_i - m_new[..., None])
        decay = jnp.exp(inter_end - m_new)
        C = decay[..., None, None] * C + jnp.einsum('bhl,bhld,bhle->bhde', s, k_i, v_i)
        n = decay[..., None] * n + jnp.einsum('bhl,bhld->bhd', s, k_i)
        return (C, n, m_new), h

    init = (jnp.zeros((b_, h_, d, d), q.dtype), jnp.zeros((b_, h_, d), q.dtype), jnp.zeros((b_, h_), q.dtype))
    _, h = lax.scan(step, init, (q, k, v, bcum, D, D_max, qk, w_end))
    return from_chunks(h)


def token_mixer(h, w_in, w_out, gdn_conv, gdn_a_log, gdn_dt_bias, gdn_norm,
                mlstm_conv, mlstm_i_bias, mlstm_f_bias, mlstm_norm):
    b_, t_, _ = h.shape
    dt = h.dtype
    f32 = jnp.float32
    qkv_a, a_a, b_a, z_a, qkv_b, qk_c, v_c, i_c, f_c, o_c = jnp.split(h @ w_in, IN_SPLITS, axis=-1)
    qa, ka, va = jnp.split(causal_conv_silu(qkv_a, gdn_conv), 3, axis=-1)
    qa = l2norm(split_heads(qa).astype(f32))
    ka = l2norm(split_heads(ka).astype(f32))
    va = split_heads(va).astype(f32)
    g = -jnp.exp(gdn_a_log.astype(f32)) * jax.nn.softplus((a_a + gdn_dt_bias).astype(f32))
    beta = jax.nn.sigmoid(b_a.astype(f32))
    oa = gated_delta_rule(qa, ka, va, g.transpose(0, 2, 1), beta.transpose(0, 2, 1)).transpose(0, 2, 1, 3)
    oa = oa * lax.rsqrt(jnp.mean(oa * oa, -1, keepdims=True) + NORM_EPS) * gdn_norm
    oa = (oa.reshape(b_, t_, W_GDN) * jax.nn.silu(z_a.astype(f32))).astype(dt)
    qb, kb, vb = jnp.split(qkv_b, 3, axis=-1)
    ob = moba_attention(split_heads(qb), split_heads(kb), split_heads(vb))
    ob = ob.transpose(0, 2, 1, 3).reshape(b_, t_, W_MOBA).astype(dt)
    qc, kc = jnp.split(causal_conv_silu(qk_c, mlstm_conv), 2, axis=-1)
    i_pre = (i_c + mlstm_i_bias).astype(f32).transpose(0, 2, 1)
    logf = jax.nn.log_sigmoid((f_c + mlstm_f_bias).astype(f32)).transpose(0, 2, 1)
    hc = mlstm_chunkwise(split_heads(qc).astype(f32), split_heads(kc).astype(f32),
                         split_heads(v_c).astype(f32), i_pre, logf).transpose(0, 2, 1, 3)
    hc = hc * jax.nn.sigmoid(o_c.astype(f32)).reshape(b_, t_, H_MLSTM, HEAD_DIM)
    mu = jnp.mean(hc, -1, keepdims=True)
    var = jnp.mean(jnp.square(hc - mu), -1, keepdims=True)
    hc = (hc - mu) * lax.rsqrt(var + NORM_EPS) * mlstm_norm.reshape(H_MLSTM, HEAD_DIM)
    hc = hc.reshape(b_, t_, W_MLSTM).astype(dt)
    return jnp.concatenate([oa, ob, hc], axis=-1) @ w_out


def setup_inputs(seed: int = 0) -> dict:
    key = jax.random.key(seed)
    ks = jax.random.split(key, 18)
    nrm = lambda k, shape, s: jax.random.normal(k, shape, jnp.float32) * s
    dt_init = jnp.exp(jax.random.uniform(ks[10], (DEPTH, H_GDN), jnp.float32,
                                         float(np.log(1e-3)), float(np.log(1e-1))))
    return {
        "x": nrm(ks[0], (BATCH, SEQ, D_MODEL), 1.0),
        "c": nrm(ks[1], (BATCH, D_MODEL), 1.0),
        "ada_w": nrm(ks[2], (DEPTH, D_MODEL, N_SUB * 3 * D_MODEL), ADA_INIT * D_MODEL ** -0.5),
        "ada_b": nrm(ks[3], (DEPTH, N_SUB * 3 * D_MODEL), 0.01),
        "ffn_w13": nrm(ks[4], (DEPTH, 2, D_MODEL, 2 * D_FF), D_MODEL ** -0.5),
        "ffn_w2": nrm(ks[5], (DEPTH, 2, D_FF, D_MODEL), BETA_INIT * D_FF ** -0.5),
        "w_in": nrm(ks[6], (DEPTH, D_MODEL, D_IN), D_MODEL ** -0.5),
        "w_out": nrm(ks[7], (DEPTH, MIX_WIDTH, D_MODEL), BETA_INIT * MIX_WIDTH ** -0.5),
        "gdn_conv": nrm(ks[8], (DEPTH, CONV_WIDTH, 3 * W_GDN), CONV_WIDTH ** -0.5),
        "gdn_a_log": jnp.log(jax.random.uniform(ks[9], (DEPTH, H_GDN), jnp.float32, 1.0, 16.0)),
        "gdn_dt_bias": dt_init + jnp.log(-jnp.expm1(-dt_init)),
        "gdn_norm": 1.0 + nrm(ks[11], (DEPTH, HEAD_DIM), 0.02),
        "mlstm_conv": nrm(ks[12], (DEPTH, CONV_WIDTH, 2 * W_MLSTM), CONV_WIDTH ** -0.5),
        "mlstm_i_bias": nrm(ks[13], (DEPTH, H_MLSTM), 0.1),
        "mlstm_f_bias": jnp.linspace(3.0, 6.0, H_MLSTM, dtype=jnp.float32)[None, :] + nrm(ks[14], (DEPTH, H_MLSTM), 0.1),
        "mlstm_norm": 1.0 + nrm(ks[15], (DEPTH, W_MLSTM), 0.02),
        "ln_g": 1.0 + nrm(ks[16], (DEPTH, N_SUB, D_MODEL), 0.02),
        "ln_b": nrm(ks[17], (DEPTH, N_SUB, D_MODEL), 0.02),
    }


def reference(x, c, ada_w, ada_b, ffn_w13, ffn_w2, w_in, w_out, gdn_conv, gdn_a_log, gdn_dt_bias,
              gdn_norm, mlstm_conv, mlstm_i_bias, mlstm_f_bias, mlstm_norm, ln_g, ln_b):
    b_ = x.shape[0]
    for l in range(DEPTH):
        mod = (jax.nn.silu(c) @ ada_w[l] + ada_b[l]).reshape(b_, N_SUB, 3, D_MODEL)
        y = swiglu(modulate(x, mod[:, 0, 0], mod[:, 0, 1]), ffn_w13[l, 0], ffn_w2[l, 0])
        x = layer_norm(ALPHA * x + 0.5 * (1 + mod[:, 0, 2])[:, None, :] * y, ln_g[l, 0], ln_b[l, 0])
        y = token_mixer(modulate(x, mod[:, 1, 0], mod[:, 1, 1]), w_in[l], w_out[l], gdn_conv[l], gdn_a_log[l],
                        gdn_dt_bias[l], gdn_norm[l], mlstm_conv[l], mlstm_i_bias[l], mlstm_f_bias[l], mlstm_norm[l])
        x = layer_norm(ALPHA * x + (1 + mod[:, 1, 2])[:, None, :] * y, ln_g[l, 1], ln_b[l, 1])
        y = swiglu(modulate(x, mod[:, 2, 0], mod[:, 2, 1]), ffn_w13[l, 1], ffn_w2[l, 1])
        x = layer_norm(ALPHA * x + 0.5 * (1 + mod[:, 2, 2])[:, None, :] * y, ln_g[l, 2], ln_b[l, 2])
    return x
```

```python
import functools

import jax
import jax.numpy as jnp
from jax import lax
from jax.experimental import pallas as pl
from jax.experimental.pallas import tpu as pltpu

F32 = jnp.float32
BF16 = jnp.bfloat16

D_MODEL = 1024
DEPTH = 2
HEAD_DIM = 64
PAIR = 2 * HEAD_DIM
H_GDN = 6
H_MOBA = 4
H_MLSTM = 6
W_GDN = H_GDN * HEAD_DIM
W_MOBA = H_MOBA * HEAD_DIM
W_MLSTM = H_MLSTM * HEAD_DIM
CONV_WIDTH = 4
CHUNK = 64
MOBA_BLOCK = 256
MOBA_TOPK = 3
D_FF = 2816
N_SUB = 3
ALPHA = (2 * DEPTH) ** 0.25
LN_EPS = 1e-5
NORM_EPS = 1e-6
SMALL = 128
CARRY_ROWS = 8

VMEM_LIMIT_BYTES = 56 * 1024 * 1024
FFN_ROWS = 512
FFN_COLS = 256
MIX_ROWS = 256
ADA_COLS = 1152


def _sigmoid(x):
    return 1.0 / (1.0 + jnp.exp(-x))


def _silu(x):
    return x * _sigmoid(x)


def _softplus(x):
    return jnp.maximum(x, 0.0) + jnp.log1p(jnp.exp(-jnp.abs(x)))


def _layer_norm(v, g, b):
    mu = jnp.mean(v, -1, keepdims=True)
    d = v - mu
    var = jnp.mean(d * d, -1, keepdims=True)
    return d * lax.rsqrt(var + LN_EPS) * g + b


def _dot(a, b):
    return jnp.dot(a.astype(BF16), b.astype(BF16), preferred_element_type=F32)


def _dot_nt(a, b):
    return lax.dot_general(a.astype(BF16), b.astype(BF16), (((1,), (1,)), ((), ())),
                           preferred_element_type=F32)


def _split(x):
    hi = x.astype(BF16)
    lo = (x - hi.astype(F32)).astype(BF16)
    return hi, lo


def _lane(shape):
    return lax.broadcasted_iota(jnp.int32, shape, len(shape) - 1)


def _row(shape):
    return lax.broadcasted_iota(jnp.int32, shape, len(shape) - 2)


def _block_diag_mask():
    return (_row((PAIR, PAIR)) < HEAD_DIM) == (_lane((PAIR, PAIR)) < HEAD_DIM)


def _bd(r2):
    left = _lane(r2.shape) < HEAD_DIM
    return jnp.concatenate([jnp.where(left, r2, 0.0), jnp.where(left, 0.0, r2)], axis=0)


def _pmm(l2, r2):
    return _dot(l2, _bd(r2))


def _pmm_nt(l2, r2):
    return _dot_nt(l2, _bd(r2))


def _pmm_tn(l2, r2):
    return jnp.where(_block_diag_mask(), _dot(l2.T, r2), 0.0)


def _segsum(x):
    ones = _block_diag_mask().astype(BF16)
    hi, lo = _split(x)
    return (jnp.dot(hi, ones, preferred_element_type=F32)
            + jnp.dot(lo, ones, preferred_element_type=F32))


def _expand(cols, ja, jb):
    left = _lane((cols.shape[0], PAIR)) < HEAD_DIM
    return jnp.where(left, cols[:, ja:ja + 1], cols[:, jb:jb + 1])


def _chunk_scan(x, combine, identity):
    pos = _row(x.shape) & (CHUNK - 1)
    step = 1
    while step < CHUNK:
        x = combine(x, jnp.where(pos >= step, pltpu.roll(x, step, axis=0), identity))
        step *= 2
    return x


def _causal_conv(z, carry_ref, w_ref):
    rows = z.shape[0]
    prev = carry_ref[...]
    carry_ref[...] = z[rows - CARRY_ROWS:, :]
    head_rows = _row(prev.shape)
    y = z * w_ref[CONV_WIDTH - 1:CONV_WIDTH, :]
    for k in range(1, CONV_WIDTH):
        zr = pltpu.roll(z, k, axis=0)
        pr = pltpu.roll(prev, k, axis=0)
        head = jnp.where(head_rows < k, pr, zr[:CARRY_ROWS])
        shifted = jnp.concatenate([head, zr[CARRY_ROWS:]], axis=0)
        y = y + shifted * w_ref[CONV_WIDTH - 1 - k:CONV_WIDTH - k, :]
    return y


def _modulated(x_ref, mod_ref):
    return (x_ref[0] * (1.0 + mod_ref[0, 1:2, :]) + mod_ref[0, 0:1, :]).astype(BF16)


def _pair_masks():
    row = _row((CHUNK, PAIR))
    col = _lane((CHUNK, PAIR)) & (HEAD_DIM - 1)
    return col <= row, col < row, col == row


def _row_form(col2, eye2):
    return jnp.sum(jnp.where(eye2, col2, 0.0), axis=0, keepdims=True)


def _ada_kernel(c_ref, w_ref, b_ref, o_ref):
    o_ref[0] = _dot(_silu(c_ref[...]), w_ref[0]) + b_ref[0]


def _ada_mod(c, ada_w, ada_b):
    depth, d, n = ada_w.shape
    rows = c.shape[0]
    b = -(-rows // 8) * 8
    c = jnp.pad(c, ((0, b - rows), (0, 0)))
    return pl.pallas_call(
        _ada_kernel,
        grid=(depth, n // ADA_COLS),
        in_specs=[
            pl.BlockSpec((b, d), lambda l, j: (0, 0)),
            pl.BlockSpec((1, d, ADA_COLS), lambda l, j: (l, 0, j)),
            pl.BlockSpec((1, 1, ADA_COLS), lambda l, j: (l, 0, j)),
        ],
        out_specs=pl.BlockSpec((1, b, ADA_COLS), lambda l, j: (l, 0, j)),
        out_shape=jax.ShapeDtypeStruct((depth, b, n), F32),
        compiler_params=pltpu.CompilerParams(vmem_limit_bytes=VMEM_LIMIT_BYTES),
        name="ada_mod",
    )(c, ada_w, ada_b.reshape(depth, 1, n))[:, :rows]


def _ffn_kernel(x_ref, mod_ref, w13_ref, w2_ref, g_ref, b_ref, o_ref):
    x = x_ref[0]
    h = _modulated(x_ref, mod_ref)
    acc = jnp.zeros(x.shape, F32)
    for c in range(D_FF // FFN_COLS):
        lo = c * FFN_COLS
        a = jnp.dot(h, w13_ref[:, lo:lo + FFN_COLS], preferred_element_type=F32)
        b = jnp.dot(h, w13_ref[:, D_FF + lo:D_FF + lo + FFN_COLS], preferred_element_type=F32)
        act = (_silu(a) * b).astype(BF16)
        acc = acc + jnp.dot(act, w2_ref[lo:lo + FFN_COLS, :], preferred_element_type=F32)
    y = ALPHA * x + (0.5 * (1.0 + mod_ref[0, 2:3, :])) * acc
    o_ref[0] = _layer_norm(y, g_ref[...], b_ref[...])


def _ffn(x, mod, w13, w2, ln_g, ln_b):
    b, t, d = x.shape
    const = lambda i, j: (0, 0)
    return pl.pallas_call(
        _ffn_kernel,
        grid=(b, t // FFN_ROWS),
        in_specs=[
            pl.BlockSpec((1, FFN_ROWS, d), lambda i, j: (i, j, 0)),
            pl.BlockSpec((1, 3, d), lambda i, j: (i, 0, 0)),
            pl.BlockSpec(w13.shape, const, pipeline_mode=pl.Buffered(1)),
            pl.BlockSpec(w2.shape, const, pipeline_mode=pl.Buffered(1)),
            pl.BlockSpec((1, d), const),
            pl.BlockSpec((1, d), const),
        ],
        out_specs=pl.BlockSpec((1, FFN_ROWS, d), lambda i, j: (i, j, 0)),
        out_shape=jax.ShapeDtypeStruct(x.shape, F32),
        compiler_params=pltpu.CompilerParams(vmem_limit_bytes=VMEM_LIMIT_BYTES),
        name="ffn",
    )(x, mod, w13, w2, ln_g.reshape(1, d), ln_b.reshape(1, d))


GDN_QKV = 3 * W_GDN
GDN_PROJ = GDN_QKV + W_GDN + SMALL


def _gdn_kernel(x_ref, mod_ref, w_ref, conv_ref, par_ref, norm_ref, o_ref, carry_ref, s_ref):
    @pl.when(pl.program_id(1) == 0)
    def _():
        carry_ref[...] = jnp.zeros_like(carry_ref)
        s_ref[...] = jnp.zeros_like(s_ref)

    proj = jnp.dot(_modulated(x_ref, mod_ref), w_ref[...], preferred_element_type=F32)
    qkv = _silu(_causal_conv(proj[:, :GDN_QKV], carry_ref, conv_ref))
    zs = proj[:, GDN_QKV + W_GDN:]
    g = -jnp.exp(par_ref[0:1, :]) * _softplus(zs + par_ref[1:2, :])
    beta = _sigmoid(zs)
    gcum = _chunk_scan(g, jnp.add, 0.0)
    incl2, strict2, eye2 = _pair_masks()
    eye_f = jnp.where(eye2, 1.0, 0.0)

    for p in range(H_GDN // 2):
        lanes = slice(p * PAIR, (p + 1) * PAIR)
        q_all = qkv[:, p * PAIR:(p + 1) * PAIR]
        k_all = qkv[:, W_GDN + p * PAIR:W_GDN + (p + 1) * PAIR]
        q_all = q_all * lax.rsqrt(_segsum(q_all * q_all) + NORM_EPS) * (HEAD_DIM ** -0.5)
        k_all = k_all * lax.rsqrt(_segsum(k_all * k_all) + NORM_EPS)
        v_all = qkv[:, 2 * W_GDN + p * PAIR:2 * W_GDN + (p + 1) * PAIR]
        z_all = proj[:, GDN_QKV + p * PAIR:GDN_QKV + (p + 1) * PAIR]
        for c in range(MIX_ROWS // CHUNK):
            rows = slice(c * CHUNK, (c + 1) * CHUNK)
            q2, k2, v2 = q_all[rows], k_all[rows], v_all[rows]
            g2 = _expand(gcum[rows], 2 * p, 2 * p + 1)
            b2 = _expand(beta[rows], H_GDN + 2 * p, H_GDN + 2 * p + 1)
            g_row = _row_form(g2, eye2)
            g_last = g2[CHUNK - 1:CHUNK, :]
            decay = jnp.exp(jnp.where(incl2, g2 - g_row, -jnp.inf))
            exp_g = jnp.exp(g2)
            kb = k2 * b2
            a2 = jnp.where(strict2, _pmm_nt(kb, k2) * decay, 0.0)
            xp = -a2
            tinv = eye_f + xp
            for _ in range(5):
                xp = _pmm(xp, xp)
                tinv = tinv + _pmm(tinv, xp)
            u2 = _pmm(tinv, v2 * b2)
            w2 = _pmm(tinv, kb * exp_g)
            attn = jnp.where(incl2, _pmm_nt(q2, k2) * decay, 0.0)
            state = s_ref[p]
            v_new = u2 - _dot(w2, state)
            o2 = _dot(q2 * exp_g, state) + _pmm(attn, v_new)
            k_dec = k2 * jnp.exp(g_last - g2)
            s_ref[p] = state * jnp.exp(g_last) + _pmm_tn(k_dec, v_new)
            o2 = o2 * lax.rsqrt(_segsum(o2 * o2) * (1.0 / HEAD_DIM) + NORM_EPS) * norm_ref[...]
            o_ref[0, rows, lanes] = (o2 * _silu(z_all[rows])).astype(o_ref.dtype)


def _gdn(x, mod, w, conv, par, norm):
    b, t, d = x.shape
    const = lambda i, j: (0, 0)
    return pl.pallas_call(
        _gdn_kernel,
        grid=(b, t // MIX_ROWS),
        in_specs=[
            pl.BlockSpec((1, MIX_ROWS, d), lambda i, j: (i, j, 0)),
            pl.BlockSpec((1, 3, d), lambda i, j: (i, 0, 0)),
            pl.BlockSpec(w.shape, const),
            pl.BlockSpec(conv.shape, const),
            pl.BlockSpec(par.shape, const),
            pl.BlockSpec(norm.shape, const),
        ],
        out_specs=pl.BlockSpec((1, MIX_ROWS, W_GDN), lambda i, j: (i, j, 0)),
        out_shape=jax.ShapeDtypeStruct((b, t, W_GDN), BF16),
        scratch_shapes=[
            pltpu.VMEM((CARRY_ROWS, GDN_QKV), F32),
            pltpu.VMEM((H_GDN // 2, PAIR, PAIR), F32),
        ],
        compiler_params=pltpu.CompilerParams(vmem_limit_bytes=VMEM_LIMIT_BYTES),
        name="gdn",
    )(x, mod, w, conv, par, norm)


ML_QK = 2 * W_MLSTM
ML_PROJ = ML_QK + 2 * W_MLSTM + SMALL


def _mlstm_kernel(x_ref, mod_ref, w_ref, conv_ref, par_ref, norm_ref, o_ref,
                  carry_ref, c_ref, n_ref, m_ref):
    @pl.when(pl.program_id(1) == 0)
    def _():
        carry_ref[...] = jnp.zeros_like(carry_ref)
        c_ref[...] = jnp.zeros_like(c_ref)
        n_ref[...] = jnp.zeros_like(n_ref)
        m_ref[...] = jnp.zeros_like(m_ref)

    proj = jnp.dot(_modulated(x_ref, mod_ref), w_ref[...], preferred_element_type=F32)
    qk = _silu(_causal_conv(proj[:, :ML_QK], carry_ref, conv_ref))
    zs = proj[:, ML_QK + 2 * W_MLSTM:] + par_ref[0:1, :]
    i_pre = zs
    logf = -_softplus(-zs)
    bcum = _chunk_scan(logf, jnp.add, 0.0)
    bcum_i = pltpu.roll(bcum, SMALL - H_MLSTM, axis=1)
    i_minus_b = i_pre - bcum_i
    run_max = _chunk_scan(i_minus_b, jnp.maximum, -jnp.inf)
    incl2, _, eye2 = _pair_masks()
    ones_bd = _block_diag_mask().astype(BF16)

    for p in range(H_MLSTM // 2):
        lanes = slice(p * PAIR, (p + 1) * PAIR)
        q_all = qk[:, p * PAIR:(p + 1) * PAIR]
        k_all = qk[:, W_MLSTM + p * PAIR:W_MLSTM + (p + 1) * PAIR] * (HEAD_DIM ** -0.5)
        v_all = proj[:, ML_QK + p * PAIR:ML_QK + (p + 1) * PAIR]
        og_all = proj[:, ML_QK + W_MLSTM + p * PAIR:ML_QK + W_MLSTM + (p + 1) * PAIR]
        for c in range(MIX_ROWS // CHUNK):
            rows = slice(c * CHUNK, (c + 1) * CHUNK)
            q2, k2, v2 = q_all[rows], k_all[rows], v_all[rows]
            b2 = _expand(bcum_i[rows], 2 * p, 2 * p + 1)
            i2 = _expand(i_pre[rows], 2 * p, 2 * p + 1)
            d_max = b2 + _expand(run_max[rows], 2 * p, 2 * p + 1)
            imb_row = _row_form(i2 - b2, eye2)
            b_last = b2[CHUNK - 1:CHUNK, :]
            c_state, n_state, m_state = c_ref[p], n_ref[p], m_ref[p]
            inter = b2 + m_state
            m_t = jnp.maximum(inter, d_max)
            a_inter = jnp.exp(inter - m_t)
            d2 = jnp.where(incl2, b2 + imb_row, -jnp.inf)
            p2 = _pmm_nt(q2, k2) * jnp.exp(d2 - m_t)
            num = a_inter * _dot(q2, c_state) + _pmm(p2, v2)
            den_hi, den_lo = _split(a_inter * q2 * n_state + p2)
            den = (jnp.dot(den_hi, ones_bd, preferred_element_type=F32)
                   + jnp.dot(den_lo, ones_bd, preferred_element_type=F32))
            hid = num / jnp.maximum(jnp.abs(den), jnp.exp(-m_t))
            inter_end = b_last + m_state
            w_end = b_last - b2 + i2
            m_new = jnp.maximum(inter_end, jnp.max(w_end, axis=0, keepdims=True))
            sk = jnp.exp(w_end - m_new) * k2
            dec = jnp.exp(inter_end - m_new)
            c_ref[p] = dec * c_state + _pmm_tn(sk, v2)
            n_ref[p] = dec * n_state + jnp.sum(sk, axis=0, keepdims=True)
            m_ref[p] = m_new
            hid = hid * _sigmoid(og_all[rows])
            mu = _segsum(hid) * (1.0 / HEAD_DIM)
            cen = hid - mu
            var = _segsum(cen * cen) * (1.0 / HEAD_DIM)
            o_ref[0, rows, lanes] = (cen * lax.rsqrt(var + NORM_EPS)
                                     * norm_ref[:, lanes]).astype(o_ref.dtype)


def _mlstm(x, mod, w, conv, par, norm):
    b, t, d = x.shape
    const = lambda i, j: (0, 0)
    return pl.pallas_call(
        _mlstm_kernel,
        grid=(b, t // MIX_ROWS),
        in_specs=[
            pl.BlockSpec((1, MIX_ROWS, d), lambda i, j: (i, j, 0)),
            pl.BlockSpec((1, 3, d), lambda i, j: (i, 0, 0)),
            pl.BlockSpec(w.shape, const),
            pl.BlockSpec(conv.shape, const),
            pl.BlockSpec(par.shape, const),
            pl.BlockSpec(norm.shape, const),
        ],
        out_specs=pl.BlockSpec((1, MIX_ROWS, W_MLSTM), lambda i, j: (i, j, 0)),
        out_shape=jax.ShapeDtypeStruct((b, t, W_MLSTM), BF16),
        scratch_shapes=[
            pltpu.VMEM((CARRY_ROWS, ML_QK), F32),
            pltpu.VMEM((H_MLSTM // 2, PAIR, PAIR), F32),
            pltpu.VMEM((H_MLSTM // 2, 1, PAIR), F32),
            pltpu.VMEM((H_MLSTM // 2, 1, PAIR), F32),
        ],
        compiler_params=pltpu.CompilerParams(vmem_limit_bytes=VMEM_LIMIT_BYTES),
        name="mlstm",
    )(x, mod, w, conv, par, norm)


def _moba_kernel(x_ref, mod_ref, w_ref, o_ref, k_ref, v_ref, kmean_ref):
    t = pl.program_id(1)
    n_blk = kmean_ref.shape[0]

    @pl.when(t == 0)
    def _():
        kmean_ref[...] = jnp.zeros_like(kmean_ref)

    proj = jnp.dot(_modulated(x_ref, mod_ref), w_ref[...], preferred_element_type=F32)
    q = proj[:, :W_MOBA]
    k = proj[:, W_MOBA:2 * W_MOBA]
    v = proj[:, 2 * W_MOBA:]
    base = pl.multiple_of(t * MOBA_BLOCK, MOBA_BLOCK)
    k_ref[pl.ds(base, MOBA_BLOCK), :] = k.astype(BF16)
    v_ref[pl.ds(base, MOBA_BLOCK), :] = v.astype(BF16)
    k_mean = jnp.mean(k, axis=0, keepdims=True)
    kmean_ref[...] = jnp.where(_row(kmean_ref.shape) == t, k_mean, kmean_ref[...])
    kmean = kmean_ref[...]

    blk = _lane((2 * MOBA_BLOCK, n_blk)).astype(F32)
    t_f = t.astype(F32)
    causal = _lane((2 * MOBA_BLOCK, MOBA_BLOCK)) <= (_row((2 * MOBA_BLOCK, MOBA_BLOCK)) & (MOBA_BLOCK - 1))
    left = _lane((MOBA_BLOCK, PAIR)) < HEAD_DIM

    for p in range(H_MOBA // 2):
        lanes = slice(p * PAIR, (p + 1) * PAIR)
        q_bd = _bd(q[:, lanes])
        q_hi, q_lo = _split(q_bd)
        m_hi, m_lo = _split(kmean[:, lanes])
        gate = _dot_nt(q_hi, m_hi) + _dot_nt(q_lo, m_hi) + _dot_nt(q_hi, m_lo)
        gate = jnp.where(blk < t_f, gate, -jnp.inf)
        picks = []
        for r in range(MOBA_TOPK):
            best = jnp.max(gate, axis=-1, keepdims=True)
            idx = jnp.min(jnp.where(gate == best, blk, float(n_blk)), axis=-1, keepdims=True)
            gate = jnp.where(blk == idx, -jnp.inf, gate)
            picks.append(jnp.where(r < t, idx, -1.0))
        qs = (q_bd * (HEAD_DIM ** -0.5)).astype(BF16)

        s = jnp.where(causal, _dot_nt(qs, k_ref[pl.ds(base, MOBA_BLOCK), lanes]), -jnp.inf)
        m_run = jnp.max(s, axis=-1, keepdims=True)
        e = jnp.exp(s - m_run)
        l_run = jnp.sum(e, axis=-1, keepdims=True)
        acc = _dot(e, v_ref[pl.ds(base, MOBA_BLOCK), lanes])

        def body(j, carry, qs=qs, picks=picks, lanes=lanes):
            m_run, l_run, acc = carry
            off = pl.multiple_of(j * MOBA_BLOCK, MOBA_BLOCK)
            j_f = j.astype(F32)
            chosen = (picks[0] == j_f) | (picks[1] == j_f) | (picks[2] == j_f)
            s = jnp.where(chosen, _dot_nt(qs, k_ref[pl.ds(off, MOBA_BLOCK), lanes]), -jnp.inf)
            m_new = jnp.maximum(m_run, jnp.max(s, axis=-1, keepdims=True))
            scale = jnp.exp(m_run - m_new)
            e = jnp.exp(s - m_new)
            l_new = l_run * scale + jnp.sum(e, axis=-1, keepdims=True)
            acc = acc * scale + _dot(e, v_ref[pl.ds(off, MOBA_BLOCK), lanes])
            return m_new, l_new, acc

        m_run, l_run, acc = lax.fori_loop(0, t, body, (m_run, l_run, acc))
        out = acc / l_run
        o_ref[0, :, lanes] = jnp.where(left, out[:MOBA_BLOCK], out[MOBA_BLOCK:]).astype(o_ref.dtype)


def _moba(x, mod, w):
    b, t, d = x.shape
    n_blk = t // MOBA_BLOCK
    assert n_blk >= MOBA_TOPK
    const = lambda i, j: (0, 0)
    return pl.pallas_call(
        _moba_kernel,
        grid=(b, n_blk),
        in_specs=[
            pl.BlockSpec((1, MOBA_BLOCK, d), lambda i, j: (i, j, 0)),
            pl.BlockSpec((1, 3, d), lambda i, j: (i, 0, 0)),
            pl.BlockSpec(w.shape, const),
        ],
        out_specs=pl.BlockSpec((1, MOBA_BLOCK, W_MOBA), lambda i, j: (i, j, 0)),
        out_shape=jax.ShapeDtypeStruct((b, t, W_MOBA), BF16),
        scratch_shapes=[
            pltpu.VMEM((t, W_MOBA), BF16),
            pltpu.VMEM((t, W_MOBA), BF16),
            pltpu.VMEM((n_blk, W_MOBA), F32),
        ],
        compiler_params=pltpu.CompilerParams(vmem_limit_bytes=VMEM_LIMIT_BYTES),
        name="moba",
    )(x, mod, w)


def _mix_out_kernel(x_ref, mod_ref, oa_ref, ob_ref, oc_ref, w_ref, g_ref, b_ref, o_ref):
    y = jnp.dot(oa_ref[0], w_ref[:W_GDN, :], preferred_element_type=F32)
    y = y + jnp.dot(ob_ref[0], w_ref[W_GDN:W_GDN + W_MOBA, :], preferred_element_type=F32)
    y = y + jnp.dot(oc_ref[0], w_ref[W_GDN + W_MOBA:, :], preferred_element_type=F32)
    v = ALPHA * x_ref[0] + (1.0 + mod_ref[0, 2:3, :]) * y
    o_ref[0] = _layer_norm(v, g_ref[...], b_ref[...])


def _mix_out(x, mod, oa, ob, oc, w_out, ln_g, ln_b):
    b, t, d = x.shape
    const = lambda i, j: (0, 0)
    tile = lambda width: pl.BlockSpec((1, FFN_ROWS, width), lambda i, j: (i, j, 0))
    return pl.pallas_call(
        _mix_out_kernel,
        grid=(b, t // FFN_ROWS),
        in_specs=[
            tile(d),
            pl.BlockSpec((1, 3, d), lambda i, j: (i, 0, 0)),
            tile(W_GDN), tile(W_MOBA), tile(W_MLSTM),
            pl.BlockSpec(w_out.shape, const),
            pl.BlockSpec((1, d), const),
            pl.BlockSpec((1, d), const),
        ],
        out_specs=tile(d),
        out_shape=jax.ShapeDtypeStruct(x.shape, F32),
        compiler_params=pltpu.CompilerParams(vmem_limit_bytes=VMEM_LIMIT_BYTES),
        name="mix_out",
    )(x, mod, oa, ob, oc, w_out, ln_g.reshape(1, d), ln_b.reshape(1, d))


def _small_cols(*cols):
    cat = jnp.concatenate(cols, axis=-1)
    return jnp.pad(cat, [(0, 0)] * (cat.ndim - 1) + [(0, SMALL - cat.shape[-1])])


def _split_w_in(w_in):
    sizes = [3 * W_GDN, H_GDN, H_GDN, W_GDN, 3 * W_MOBA,
             2 * W_MLSTM, W_MLSTM, H_MLSTM, H_MLSTM, W_MLSTM]
    parts, lo = [], 0
    for s in sizes:
        parts.append(w_in[:, lo:lo + s])
        lo += s
    qkv_a, a_a, b_a, z_a, qkv_b, qk_c, v_c, i_c, f_c, o_c = parts
    w_gdn = jnp.concatenate([qkv_a, z_a, _small_cols(a_a, b_a)], axis=1).astype(BF16)
    w_moba = qkv_b.astype(BF16)
    w_ml = jnp.concatenate([qk_c, v_c, o_c, _small_cols(i_c, f_c)], axis=1).astype(BF16)
    return w_gdn, w_moba, w_ml


def kernel(x, c, ada_w, ada_b, ffn_w13, ffn_w2, w_in, w_out, gdn_conv, gdn_a_log, gdn_dt_bias, gdn_norm,
           mlstm_conv, mlstm_i_bias, mlstm_f_bias, mlstm_norm, ln_g, ln_b):
    b = x.shape[0]
    depth = ada_w.shape[0]
    mod_all = _ada_mod(c, ada_w, ada_b).reshape(depth, b, N_SUB, 3, D_MODEL)
    for l in range(depth):
        mod = mod_all[l]
        w_gdn, w_moba, w_ml = _split_w_in(w_in[l])
        gdn_par = jnp.stack([_small_cols(gdn_a_log[l]), _small_cols(gdn_dt_bias[l])])
        gdn_par = jnp.pad(gdn_par, ((0, 6), (0, 0)))
        ml_par = jnp.pad(_small_cols(mlstm_i_bias[l], mlstm_f_bias[l])[None, :], ((0, 7), (0, 0)))
        gdn_norm2 = jnp.tile(gdn_norm[l], 2).reshape(1, PAIR)

        x = _ffn(x, mod[:, 0], ffn_w13[l, 0].astype(BF16), ffn_w2[l, 0].astype(BF16), ln_g[l, 0], ln_b[l, 0])
        oa = _gdn(x, mod[:, 1], w_gdn, gdn_conv[l], gdn_par, gdn_norm2)
        ob = _moba(x, mod[:, 1], w_moba)
        oc = _mlstm(x, mod[:, 1], w_ml, mlstm_conv[l], ml_par, mlstm_norm[l].reshape(1, W_MLSTM))
        x = _mix_out(x, mod[:, 1], oa, ob, oc, w_out[l].astype(BF16), ln_g[l, 1], ln_b[l, 1])
        x = _ffn(x, mod[:, 2], ffn_w13[l, 1].astype(BF16), ffn_w2[l, 1].astype(BF16), ln_g[l, 2], ln_b[l, 2])
    return x
```

```python
import functools

import jax
import jax.numpy as jnp
from jax import lax
from jax.experimental import pallas as pl
from jax.experimental.pallas import tpu as pltpu

F32 = jnp.float32
BF16 = jnp.bfloat16

D_MODEL = 1024
DEPTH = 2
HEAD_DIM = 64
PAIR = 2 * HEAD_DIM
H_GDN = 6
H_MOBA = 4
H_MLSTM = 6
W_GDN = H_GDN * HEAD_DIM
W_MOBA = H_MOBA * HEAD_DIM
W_MLSTM = H_MLSTM * HEAD_DIM
CONV_WIDTH = 4
CHUNK = 64
MOBA_BLOCK = 256
MOBA_TOPK = 3
MOBA_STEP = 2
D_FF = 2816
N_SUB = 3
ALPHA = (2 * DEPTH) ** 0.25
LN_EPS = 1e-5
NORM_EPS = 1e-6
LOG2_E = 1.4426950408889634
SMALL = 128
CARRY_ROWS = 8

VMEM_LIMIT_BYTES = 56 * 1024 * 1024
FFN_ROWS = 512
FFN_COLS = 256
MIX_ROWS = 256
ADA_COLS = 1152


def _sigmoid(x):
    return 1.0 / (1.0 + jnp.exp(-x))


def _silu(x):
    return x * _sigmoid(x)


def _softplus(x):
    return jnp.maximum(x, 0.0) + jnp.log1p(jnp.exp(-jnp.abs(x)))


def _layer_norm(v, g, b):
    mu = jnp.mean(v, -1, keepdims=True)
    d = v - mu
    var = jnp.mean(d * d, -1, keepdims=True)
    return d * lax.rsqrt(var + LN_EPS) * g + b


def _dot(a, b):
    return jnp.dot(a.astype(BF16), b.astype(BF16), preferred_element_type=F32)


def _dot_nt(a, b):
    return lax.dot_general(a.astype(BF16), b.astype(BF16), (((1,), (1,)), ((), ())),
                           preferred_element_type=F32)


def _split(x):
    hi = x.astype(BF16)
    lo = (x - hi.astype(F32)).astype(BF16)
    return hi, lo


def _lane(shape):
    return lax.broadcasted_iota(jnp.int32, shape, len(shape) - 1)


def _row(shape):
    return lax.broadcasted_iota(jnp.int32, shape, len(shape) - 2)


def _block_diag_mask():
    return (_row((PAIR, PAIR)) < HEAD_DIM) == (_lane((PAIR, PAIR)) < HEAD_DIM)


def _bd(r2):
    left = _lane(r2.shape) < HEAD_DIM
    return jnp.concatenate([jnp.where(left, r2, 0.0), jnp.where(left, 0.0, r2)], axis=0)


def _pmm(l2, r2):
    return _dot(l2, _bd(r2))


def _pmm_nt(l2, r2):
    return _dot_nt(l2, _bd(r2))


def _pmm_tn(l2, r2):
    return jnp.where(_block_diag_mask(), _dot(l2.T, r2), 0.0)


def _segsum(x):
    ones = _block_diag_mask().astype(BF16)
    hi, lo = _split(x)
    return (jnp.dot(hi, ones, preferred_element_type=F32)
            + jnp.dot(lo, ones, preferred_element_type=F32))


def _expand(cols, ja, jb):
    left = _lane((cols.shape[0], PAIR)) < HEAD_DIM
    return jnp.where(left, cols[:, ja:ja + 1], cols[:, jb:jb + 1])


def _chunk_scan(x, combine, identity):
    pos = _row(x.shape) & (CHUNK - 1)
    step = 1
    while step < CHUNK:
        x = combine(x, jnp.where(pos >= step, pltpu.roll(x, step, axis=0), identity))
        step *= 2
    return x


def _causal_conv(z, carry_ref, w_ref):
    rows = z.shape[0]
    prev = carry_ref[...]
    carry_ref[...] = z[rows - CARRY_ROWS:, :]
    head_rows = _row(prev.shape)
    y = z * w_ref[CONV_WIDTH - 1:CONV_WIDTH, :]
    for k in range(1, CONV_WIDTH):
        zr = pltpu.roll(z, k, axis=0)
        pr = pltpu.roll(prev, k, axis=0)
        head = jnp.where(head_rows < k, pr, zr[:CARRY_ROWS])
        shifted = jnp.concatenate([head, zr[CARRY_ROWS:]], axis=0)
        y = y + shifted * w_ref[CONV_WIDTH - 1 - k:CONV_WIDTH - k, :]
    return y


def _modulated(x_ref, mod_ref):
    return (x_ref[0] * (1.0 + mod_ref[0, 1:2, :]) + mod_ref[0, 0:1, :]).astype(BF16)


def _pair_masks():
    row = _row((CHUNK, PAIR))
    col = _lane((CHUNK, PAIR)) & (HEAD_DIM - 1)
    return col <= row, col < row, col == row


def _row_form(col2, eye2):
    return jnp.sum(jnp.where(eye2, col2, 0.0), axis=0, keepdims=True)


def _ada_kernel(c_ref, w_ref, b_ref, o_ref):
    o_ref[0] = _dot(_silu(c_ref[...]), w_ref[0]) + b_ref[0]


def _ada_mod(c, ada_w, ada_b):
    depth, d, n = ada_w.shape
    rows = c.shape[0]
    b = -(-rows // 8) * 8
    c = jnp.pad(c, ((0, b - rows), (0, 0)))
    return pl.pallas_call(
        _ada_kernel,
        grid=(depth, n // ADA_COLS),
        in_specs=[
            pl.BlockSpec((b, d), lambda l, j: (0, 0)),
            pl.BlockSpec((1, d, ADA_COLS), lambda l, j: (l, 0, j)),
            pl.BlockSpec((1, 1, ADA_COLS), lambda l, j: (l, 0, j)),
        ],
        out_specs=pl.BlockSpec((1, b, ADA_COLS), lambda l, j: (l, 0, j)),
        out_shape=jax.ShapeDtypeStruct((depth, b, n), F32),
        compiler_params=pltpu.CompilerParams(vmem_limit_bytes=VMEM_LIMIT_BYTES),
        name="ada_mod",
    )(c, ada_w, ada_b.reshape(depth, 1, n))[:, :rows]


def _ffn_kernel(x_ref, mod_ref, w13_ref, w2_ref, g_ref, b_ref, o_ref):
    x = x_ref[0]
    h = _modulated(x_ref, mod_ref)
    acc = jnp.zeros(x.shape, F32)
    for c in range(D_FF // FFN_COLS):
        lo = c * FFN_COLS
        a = jnp.dot(h, w13_ref[:, lo:lo + FFN_COLS], preferred_element_type=F32)
        b = jnp.dot(h, w13_ref[:, D_FF + lo:D_FF + lo + FFN_COLS], preferred_element_type=F32)
        act = (_silu(a) * b).astype(BF16)
        acc = acc + jnp.dot(act, w2_ref[lo:lo + FFN_COLS, :], preferred_element_type=F32)
    y = ALPHA * x + (0.5 * (1.0 + mod_ref[0, 2:3, :])) * acc
    o_ref[0] = _layer_norm(y, g_ref[...], b_ref[...])


def _ffn(x, mod, w13, w2, ln_g, ln_b):
    b, t, d = x.shape
    const = lambda i, j: (0, 0)
    return pl.pallas_call(
        _ffn_kernel,
        grid=(b, t // FFN_ROWS),
        in_specs=[
            pl.BlockSpec((1, FFN_ROWS, d), lambda i, j: (i, j, 0)),
            pl.BlockSpec((1, 3, d), lambda i, j: (i, 0, 0)),
            pl.BlockSpec(w13.shape, const, pipeline_mode=pl.Buffered(1)),
            pl.BlockSpec(w2.shape, const, pipeline_mode=pl.Buffered(1)),
            pl.BlockSpec((1, d), const),
            pl.BlockSpec((1, d), const),
        ],
        out_specs=pl.BlockSpec((1, FFN_ROWS, d), lambda i, j: (i, j, 0)),
        out_shape=jax.ShapeDtypeStruct(x.shape, F32),
        compiler_params=pltpu.CompilerParams(vmem_limit_bytes=VMEM_LIMIT_BYTES),
        name="ffn",
    )(x, mod, w13, w2, ln_g.reshape(1, d), ln_b.reshape(1, d))


GDN_QKV = 3 * W_GDN
GDN_PROJ = GDN_QKV + W_GDN + SMALL


def _gdn_kernel(x_ref, mod_ref, w_ref, conv_ref, par_ref, norm_ref, o_ref, carry_ref, s_ref):
    @pl.when(pl.program_id(1) == 0)
    def _():
        carry_ref[...] = jnp.zeros_like(carry_ref)
        s_ref[...] = jnp.zeros_like(s_ref)

    proj = jnp.dot(_modulated(x_ref, mod_ref), w_ref[...], preferred_element_type=F32)
    qkv = _silu(_causal_conv(proj[:, :GDN_QKV], carry_ref, conv_ref))
    zs = proj[:, GDN_QKV + W_GDN:]
    g = -jnp.exp(par_ref[0:1, :]) * _softplus(zs + par_ref[1:2, :])
    beta = _sigmoid(zs)
    gcum = _chunk_scan(g, jnp.add, 0.0)
    incl2, strict2, eye2 = _pair_masks()
    eye_f = jnp.where(eye2, 1.0, 0.0)
    pairs = range(H_GDN // 2)
    chunks = range(MIX_ROWS // CHUNK)
    units = [(p, c) for p in pairs for c in chunks]

    q_n, k_n = [], []
    for p in pairs:
        q_all = qkv[:, p * PAIR:(p + 1) * PAIR]
        k_all = qkv[:, W_GDN + p * PAIR:W_GDN + (p + 1) * PAIR]
        q_n.append(q_all * lax.rsqrt(_segsum(q_all * q_all) + NORM_EPS) * (HEAD_DIM ** -0.5))
        k_n.append(k_all * lax.rsqrt(_segsum(k_all * k_all) + NORM_EPS))
    st = {}
    for p, c in units:
        rows = slice(c * CHUNK, (c + 1) * CHUNK)
        q2, k2 = q_n[p][rows], k_n[p][rows]
        v2 = qkv[rows, 2 * W_GDN + p * PAIR:2 * W_GDN + (p + 1) * PAIR]
        g2 = _expand(gcum[rows], 2 * p, 2 * p + 1)
        b2 = _expand(beta[rows], H_GDN + 2 * p, H_GDN + 2 * p + 1)
        g_last = g2[CHUNK - 1:CHUNK, :]
        decay = jnp.exp(jnp.where(incl2, g2 - _row_form(g2, eye2), -jnp.inf))
        exp_g = jnp.exp(g2)
        kb = k2 * b2
        gram = _dot_nt(jnp.concatenate([kb, q2], axis=0), _bd(k2))
        st[p, c] = dict(
            xp=-jnp.where(strict2, gram[:CHUNK] * decay, 0.0),
            attn=jnp.where(incl2, gram[CHUNK:] * decay, 0.0),
            rhs=jnp.concatenate([_bd(v2 * b2), _bd(kb * exp_g)], axis=1),
            q_dec=q2 * exp_g,
            k_dec=k2 * jnp.exp(g_last - g2),
            g_end=jnp.exp(g_last))

    tinv = {u: eye_f + st[u]["xp"] for u in units}
    for _ in range(5):
        for u in units:
            st[u]["xp"] = _pmm(st[u]["xp"], st[u]["xp"])
        for u in units:
            tinv[u] = tinv[u] + _pmm(tinv[u], st[u]["xp"])
    for u in units:
        uw = _dot(tinv[u], st[u]["rhs"])
        st[u]["u"], st[u]["w"] = uw[:, :PAIR], uw[:, PAIR:]

    for c in chunks:
        rows = slice(c * CHUNK, (c + 1) * CHUNK)
        for p in pairs:
            e = st[p, c]
            state = s_ref[p]
            ws_qs = _dot(jnp.concatenate([e["w"], e["q_dec"]], axis=0), state)
            v_new = e["u"] - ws_qs[:CHUNK]
            o2 = ws_qs[CHUNK:] + _pmm(e["attn"], v_new)
            s_ref[p] = state * e["g_end"] + _pmm_tn(e["k_dec"], v_new)
            o2 = o2 * lax.rsqrt(_segsum(o2 * o2) * (1.0 / HEAD_DIM) + NORM_EPS) * norm_ref[...]
            z2 = proj[rows, GDN_QKV + p * PAIR:GDN_QKV + (p + 1) * PAIR]
            o_ref[0, rows, p * PAIR:(p + 1) * PAIR] = (o2 * _silu(z2)).astype(o_ref.dtype)


def _gdn(x, mod, w, conv, par, norm):
    b, t, d = x.shape
    const = lambda i, j: (0, 0)
    return pl.pallas_call(
        _gdn_kernel,
        grid=(b, t // MIX_ROWS),
        in_specs=[
            pl.BlockSpec((1, MIX_ROWS, d), lambda i, j: (i, j, 0)),
            pl.BlockSpec((1, 3, d), lambda i, j: (i, 0, 0)),
            pl.BlockSpec(w.shape, const),
            pl.BlockSpec(conv.shape, const),
            pl.BlockSpec(par.shape, const),
            pl.BlockSpec(norm.shape, const),
        ],
        out_specs=pl.BlockSpec((1, MIX_ROWS, W_GDN), lambda i, j: (i, j, 0)),
        out_shape=jax.ShapeDtypeStruct((b, t, W_GDN), BF16),
        scratch_shapes=[
            pltpu.VMEM((CARRY_ROWS, GDN_QKV), F32),
            pltpu.VMEM((H_GDN // 2, PAIR, PAIR), F32),
        ],
        compiler_params=pltpu.CompilerParams(vmem_limit_bytes=VMEM_LIMIT_BYTES),
        name="gdn",
    )(x, mod, w, conv, par, norm)


ML_QK = 2 * W_MLSTM
ML_PROJ = ML_QK + 2 * W_MLSTM + SMALL


def _mlstm_kernel(x_ref, mod_ref, w_ref, conv_ref, par_ref, norm_ref, o_ref,
                  carry_ref, c_ref, n_ref, m_ref):
    @pl.when(pl.program_id(1) == 0)
    def _():
        carry_ref[...] = jnp.zeros_like(carry_ref)
        c_ref[...] = jnp.zeros_like(c_ref)
        n_ref[...] = jnp.zeros_like(n_ref)
        m_ref[...] = jnp.zeros_like(m_ref)

    proj = jnp.dot(_modulated(x_ref, mod_ref), w_ref[...], preferred_element_type=F32)
    qk = _silu(_causal_conv(proj[:, :ML_QK], carry_ref, conv_ref))
    zs = proj[:, ML_QK + 2 * W_MLSTM:] + par_ref[0:1, :]
    i_pre = zs
    logf = -_softplus(-zs)
    bcum = _chunk_scan(logf, jnp.add, 0.0)
    bcum_i = pltpu.roll(bcum, SMALL - H_MLSTM, axis=1)
    i_minus_b = i_pre - bcum_i
    run_max = _chunk_scan(i_minus_b, jnp.maximum, -jnp.inf)
    incl2, _, eye2 = _pair_masks()
    ones_bd = _block_diag_mask().astype(BF16)
    pairs = range(H_MLSTM // 2)
    chunks = range(MIX_ROWS // CHUNK)
    units = [(p, c) for p in pairs for c in chunks]

    def operands(p, c):
        rows = slice(c * CHUNK, (c + 1) * CHUNK)
        q2 = qk[rows, p * PAIR:(p + 1) * PAIR]
        k2 = qk[rows, W_MLSTM + p * PAIR:W_MLSTM + (p + 1) * PAIR] * (HEAD_DIM ** -0.5)
        v2 = proj[rows, ML_QK + p * PAIR:ML_QK + (p + 1) * PAIR]
        return q2, k2, v2

    st = {}
    for p in pairs:
        m_state = m_ref[p]
        for c in chunks:
            rows = slice(c * CHUNK, (c + 1) * CHUNK)
            b2 = _expand(bcum_i[rows], 2 * p, 2 * p + 1)
            i2 = _expand(i_pre[rows], 2 * p, 2 * p + 1)
            d_max = b2 + _expand(run_max[rows], 2 * p, 2 * p + 1)
            b_last = b2[CHUNK - 1:CHUNK, :]
            w_end = b_last - b2 + i2
            inter = b2 + m_state
            m_t = jnp.maximum(inter, d_max)
            inter_end = b_last + m_state
            m_new = jnp.maximum(inter_end, jnp.max(w_end, axis=0, keepdims=True))
            d2 = jnp.where(incl2, b2 + _row_form(i2 - b2, eye2), -jnp.inf)
            st[p, c] = dict(m_t=m_t, a_inter=jnp.exp(inter - m_t), decay=jnp.exp(d2 - m_t),
                            s=jnp.exp(w_end - m_new), dec=jnp.exp(inter_end - m_new))
            m_state = m_new
        m_ref[p] = m_state

    for p, c in units:
        q2, k2, v2 = operands(p, c)
        e = st[p, c]
        e["p2"] = _pmm_nt(q2, k2) * e["decay"]
        sk = e["s"] * k2
        e["kv"] = _pmm_tn(sk, v2)
        e["ksum"] = jnp.sum(sk, axis=0, keepdims=True)
    for p, c in units:
        st[p, c]["intra"] = _pmm(st[p, c]["p2"], operands(p, c)[2])

    for p in pairs:
        c_state, n_state = c_ref[p], n_ref[p]
        for c in chunks:
            e = st[p, c]
            e["c_in"], e["n_in"] = c_state, n_state
            c_state = e["dec"] * c_state + e["kv"]
            n_state = e["dec"] * n_state + e["ksum"]
        c_ref[p], n_ref[p] = c_state, n_state

    for p, c in units:
        rows = slice(c * CHUNK, (c + 1) * CHUNK)
        lanes = slice(p * PAIR, (p + 1) * PAIR)
        q2 = operands(p, c)[0]
        e = st[p, c]
        num = e["a_inter"] * _dot(q2, e["c_in"]) + e["intra"]
        den_hi, den_lo = _split(e["a_inter"] * q2 * e["n_in"] + e["p2"])
        den = (jnp.dot(den_hi, ones_bd, preferred_element_type=F32)
               + jnp.dot(den_lo, ones_bd, preferred_element_type=F32))
        hid = num / jnp.maximum(jnp.abs(den), jnp.exp(-e["m_t"]))
        hid = hid * _sigmoid(proj[rows, ML_QK + W_MLSTM + p * PAIR:ML_QK + W_MLSTM + (p + 1) * PAIR])
        mu = _segsum(hid) * (1.0 / HEAD_DIM)
        cen = hid - mu
        var = _segsum(cen * cen) * (1.0 / HEAD_DIM)
        o_ref[0, rows, lanes] = (cen * lax.rsqrt(var + NORM_EPS) * norm_ref[:, lanes]).astype(o_ref.dtype)


def _mlstm(x, mod, w, conv, par, norm):
    b, t, d = x.shape
    const = lambda i, j: (0, 0)
    return pl.pallas_call(
        _mlstm_kernel,
        grid=(b, t // MIX_ROWS),
        in_specs=[
            pl.BlockSpec((1, MIX_ROWS, d), lambda i, j: (i, j, 0)),
            pl.BlockSpec((1, 3, d), lambda i, j: (i, 0, 0)),
            pl.BlockSpec(w.shape, const),
            pl.BlockSpec(conv.shape, const),
            pl.BlockSpec(par.shape, const),
            pl.BlockSpec(norm.shape, const),
        ],
        out_specs=pl.BlockSpec((1, MIX_ROWS, W_MLSTM), lambda i, j: (i, j, 0)),
        out_shape=jax.ShapeDtypeStruct((b, t, W_MLSTM), BF16),
        scratch_shapes=[
            pltpu.VMEM((CARRY_ROWS, ML_QK), F32),
            pltpu.VMEM((H_MLSTM // 2, PAIR, PAIR), F32),
            pltpu.VMEM((H_MLSTM // 2, 1, PAIR), F32),
            pltpu.VMEM((H_MLSTM // 2, 1, PAIR), F32),
        ],
        compiler_params=pltpu.CompilerParams(vmem_limit_bytes=VMEM_LIMIT_BYTES),
        name="mlstm",
    )(x, mod, w, conv, par, norm)


def _moba_kernel(x_ref, mod_ref, w_ref, o_ref, k_ref, vt_ref, kmean_ref):
    t = pl.program_id(1)
    n_blk = kmean_ref.shape[0]

    @pl.when(t == 0)
    def _():
        kmean_ref[...] = jnp.zeros_like(kmean_ref)

    proj = jnp.dot(_modulated(x_ref, mod_ref), w_ref[...], preferred_element_type=F32)
    q = proj[:, :W_MOBA]
    k = proj[:, W_MOBA:2 * W_MOBA]
    v = proj[:, 2 * W_MOBA:]
    k_ref[t] = k.astype(BF16)
    vt_ref[t] = v.T.astype(BF16)
    k_mean = jnp.mean(k, axis=0, keepdims=True)
    kmean_ref[...] = jnp.where(_row(kmean_ref.shape) == t, k_mean, kmean_ref[...])
    kmean = kmean_ref[...]

    n_q = 2 * MOBA_BLOCK
    blk = _row((n_blk, n_q)).astype(F32)
    t_f = t.astype(F32)
    causal = _row((MOBA_BLOCK, n_q)) <= (_lane((MOBA_BLOCK, n_q)) & (MOBA_BLOCK - 1))
    left = _lane((MOBA_BLOCK, PAIR)) < HEAD_DIM

    pairs = range(H_MOBA // 2)
    qs_all, picks_all, init = [], [], []
    for p in pairs:
        lanes = slice(p * PAIR, (p + 1) * PAIR)
        q_bd = _bd(q[:, lanes])
        q_hi, q_lo = _split(q_bd)
        m_hi, m_lo = _split(kmean[:, lanes])
        gate = _dot_nt(m_hi, q_hi) + _dot_nt(m_hi, q_lo) + _dot_nt(m_lo, q_hi)
        gate = jnp.where(blk < t_f, gate, -jnp.inf)
        picks = []
        for r in range(MOBA_TOPK):
            best = jnp.max(gate, axis=0, keepdims=True)
            idx = jnp.min(jnp.where(gate == best, blk, float(n_blk)), axis=0, keepdims=True)
            gate = jnp.where(blk == idx, -jnp.inf, gate)
            picks.append(jnp.where(r < t, idx, -1.0))
        qs = (q_bd * (HEAD_DIM ** -0.5 * LOG2_E)).astype(BF16)
        s = jnp.where(causal, _dot_nt(k[:, lanes], qs), -jnp.inf)
        m_run = jnp.max(s, axis=0, keepdims=True)
        e = jnp.exp2(s - m_run)
        init.append((m_run, jnp.sum(e, axis=0, keepdims=True), _dot(v[:, lanes].T, e)))
        qs_all.append(qs)
        picks_all.append(picks)

    def body(i, carry):
        new = []
        for p in pairs:
            m_run, l_run, acc = carry[p]
            rows = slice(p * PAIR, (p + 1) * PAIR)
            parts = []
            m_new = m_run
            for sub in range(MOBA_STEP):
                j = i * MOBA_STEP + sub
                j_f = j.astype(F32)
                chosen = ((picks_all[p][0] == j_f) | (picks_all[p][1] == j_f)
                          | (picks_all[p][2] == j_f))
                bias = jnp.where(chosen, 0.0, -jnp.inf)
                j_ld = jnp.minimum(j, t)
                s = _dot_nt(k_ref[j_ld, :, rows], qs_all[p])
                m_new = jnp.maximum(m_new, jnp.max(s, axis=0, keepdims=True) + bias)
                parts.append((s, bias, j_ld))
            scale = jnp.exp2(m_run - m_new)
            l_run = l_run * scale
            acc = acc * scale
            for s, bias, j_ld in parts:
                e = jnp.exp2(s + (bias - m_new))
                l_run = l_run + jnp.sum(e, axis=0, keepdims=True)
                acc = acc + _dot(vt_ref[j_ld, rows, :], e)
            new.append((m_new, l_run, acc))
        return tuple(new)

    final = lax.fori_loop(0, (t + MOBA_STEP - 1) // MOBA_STEP, body, tuple(init))
    for p in pairs:
        _, l_run, acc = final[p]
        out = (acc / l_run).T
        o_ref[0, :, p * PAIR:(p + 1) * PAIR] = jnp.where(
            left, out[:MOBA_BLOCK], out[MOBA_BLOCK:]).astype(o_ref.dtype)


def _moba(x, mod, w):
    b, t, d = x.shape
    n_blk = t // MOBA_BLOCK
    assert n_blk >= MOBA_TOPK
    const = lambda i, j: (0, 0)
    return pl.pallas_call(
        _moba_kernel,
        grid=(b, n_blk),
        in_specs=[
            pl.BlockSpec((1, MOBA_BLOCK, d), lambda i, j: (i, j, 0)),
            pl.BlockSpec((1, 3, d), lambda i, j: (i, 0, 0)),
            pl.BlockSpec(w.shape, const),
        ],
        out_specs=pl.BlockSpec((1, MOBA_BLOCK, W_MOBA), lambda i, j: (i, j, 0)),
        out_shape=jax.ShapeDtypeStruct((b, t, W_MOBA), BF16),
        scratch_shapes=[
            pltpu.VMEM((n_blk, MOBA_BLOCK, W_MOBA), BF16),
            pltpu.VMEM((n_blk, W_MOBA, MOBA_BLOCK), BF16),
            pltpu.VMEM((n_blk, W_MOBA), F32),
        ],
        compiler_params=pltpu.CompilerParams(vmem_limit_bytes=VMEM_LIMIT_BYTES),
        name="moba",
    )(x, mod, w)


def _mix_out_kernel(x_ref, mod_ref, oa_ref, ob_ref, oc_ref, w_ref, g_ref, b_ref, o_ref):
    y = jnp.dot(oa_ref[0], w_ref[:W_GDN, :], preferred_element_type=F32)
    y = y + jnp.dot(ob_ref[0], w_ref[W_GDN:W_GDN + W_MOBA, :], preferred_element_type=F32)
    y = y + jnp.dot(oc_ref[0], w_ref[W_GDN + W_MOBA:, :], preferred_element_type=F32)
    v = ALPHA * x_ref[0] + (1.0 + mod_ref[0, 2:3, :]) * y
    o_ref[0] = _layer_norm(v, g_ref[...], b_ref[...])


def _mix_out(x, mod, oa, ob, oc, w_out, ln_g, ln_b):
    b, t, d = x.shape
    const = lambda i, j: (0, 0)
    tile = lambda width: pl.BlockSpec((1, FFN_ROWS, width), lambda i, j: (i, j, 0))
    return pl.pallas_call(
        _mix_out_kernel,
        grid=(b, t // FFN_ROWS),
        in_specs=[
            tile(d),
            pl.BlockSpec((1, 3, d), lambda i, j: (i, 0, 0)),
            tile(W_GDN), tile(W_MOBA), tile(W_MLSTM),
            pl.BlockSpec(w_out.shape, const),
            pl.BlockSpec((1, d), const),
            pl.BlockSpec((1, d), const),
        ],
        out_specs=tile(d),
        out_shape=jax.ShapeDtypeStruct(x.shape, F32),
        compiler_params=pltpu.CompilerParams(vmem_limit_bytes=VMEM_LIMIT_BYTES),
        name="mix_out",
    )(x, mod, oa, ob, oc, w_out, ln_g.reshape(1, d), ln_b.reshape(1, d))


def _small_cols(*cols):
    cat = jnp.concatenate(cols, axis=-1)
    return jnp.pad(cat, [(0, 0)] * (cat.ndim - 1) + [(0, SMALL - cat.shape[-1])])


def _split_w_in(w_in):
    sizes = [3 * W_GDN, H_GDN, H_GDN, W_GDN, 3 * W_MOBA,
             2 * W_MLSTM, W_MLSTM, H_MLSTM, H_MLSTM, W_MLSTM]
    parts, lo = [], 0
    for s in sizes:
        parts.append(w_in[:, lo:lo + s])
        lo += s
    qkv_a, a_a, b_a, z_a, qkv_b, qk_c, v_c, i_c, f_c, o_c = parts
    w_gdn = jnp.concatenate([qkv_a, z_a, _small_cols(a_a, b_a)], axis=1).astype(BF16)
    w_moba = qkv_b.astype(BF16)
    w_ml = jnp.concatenate([qk_c, v_c, o_c, _small_cols(i_c, f_c)], axis=1).astype(BF16)
    return w_gdn, w_moba, w_ml


def kernel(x, c, ada_w, ada_b, ffn_w13, ffn_w2, w_in, w_out, gdn_conv, gdn_a_log, gdn_dt_bias, gdn_norm,
           mlstm_conv, mlstm_i_bias, mlstm_f_bias, mlstm_norm, ln_g, ln_b):
    b = x.shape[0]
    depth = ada_w.shape[0]
    mod_all = _ada_mod(c, ada_w, ada_b).reshape(depth, b, N_SUB, 3, D_MODEL)
    for l in range(depth):
        mod = mod_all[l]
        w_gdn, w_moba, w_ml = _split_w_in(w_in[l])
        gdn_par = jnp.stack([_small_cols(gdn_a_log[l]), _small_cols(gdn_dt_bias[l])])
        gdn_par = jnp.pad(gdn_par, ((0, 6), (0, 0)))
        ml_par = jnp.pad(_small_cols(mlstm_i_bias[l], mlstm_f_bias[l])[None, :], ((0, 7), (0, 0)))
        gdn_norm2 = jnp.tile(gdn_norm[l], 2).reshape(1, PAIR)

        x = _ffn(x, mod[:, 0], ffn_w13[l, 0].astype(BF16), ffn_w2[l, 0].astype(BF16), ln_g[l, 0], ln_b[l, 0])
        oa = _gdn(x, mod[:, 1], w_gdn, gdn_conv[l], gdn_par, gdn_norm2)
        ob = _moba(x, mod[:, 1], w_moba)
        oc = _mlstm(x, mod[:, 1], w_ml, mlstm_conv[l], ml_par, mlstm_norm[l].reshape(1, W_MLSTM))
        x = _mix_out(x, mod[:, 1], oa, ob, oc, w_out[l].astype(BF16), ln_g[l, 1], ln_b[l, 1])
        x = _ffn(x, mod[:, 2], ffn_w13[l, 1].astype(BF16), ffn_w2[l, 1].astype(BF16), ln_g[l, 2], ln_b[l, 2])
    return x
```

```python
import functools

import jax
import jax.numpy as jnp
from jax import lax
from jax.experimental import pallas as pl
from jax.experimental.pallas import tpu as pltpu

F32 = jnp.float32
BF16 = jnp.bfloat16

D_MODEL = 1024
DEPTH = 2
HEAD_DIM = 64
PAIR = 2 * HEAD_DIM
H_GDN = 6
H_MOBA = 4
H_MLSTM = 6
W_GDN = H_GDN * HEAD_DIM
W_MOBA = H_MOBA * HEAD_DIM
W_MLSTM = H_MLSTM * HEAD_DIM
CONV_WIDTH = 4
CHUNK = 64
MOBA_BLOCK = 256
MOBA_TOPK = 3
VT_ROWS = PAIR + 16
D_FF = 2816
N_SUB = 3
ALPHA = (2 * DEPTH) ** 0.25
LN_EPS = 1e-5
NORM_EPS = 1e-6
LOG2_E = 1.4426950408889634
SMALL = 128
CARRY_ROWS = 8

VMEM_LIMIT_BYTES = 56 * 1024 * 1024
FFN_ROWS = 512
FFN_COLS = 256
MIX_ROWS = 512
ADA_COLS = 1152


def _sigmoid(x):
    return 1.0 / (1.0 + jnp.exp(-x))


def _silu(x):
    return x * _sigmoid(x)


def _softplus(x):
    return jnp.maximum(x, 0.0) + jnp.log1p(jnp.exp(-jnp.abs(x)))


def _layer_norm(v, g, b):
    mu = jnp.mean(v, -1, keepdims=True)
    d = v - mu
    var = jnp.mean(d * d, -1, keepdims=True)
    return d * lax.rsqrt(var + LN_EPS) * g + b


def _dot(a, b):
    return jnp.dot(a.astype(BF16), b.astype(BF16), preferred_element_type=F32)


def _dot_nt(a, b):
    return lax.dot_general(a.astype(BF16), b.astype(BF16), (((1,), (1,)), ((), ())),
                           preferred_element_type=F32)


def _split(x):
    hi = x.astype(BF16)
    lo = (x - hi.astype(F32)).astype(BF16)
    return hi, lo


def _lane(shape):
    return lax.broadcasted_iota(jnp.int32, shape, len(shape) - 1)


def _row(shape):
    return lax.broadcasted_iota(jnp.int32, shape, len(shape) - 2)


def _block_diag_mask():
    return (_row((PAIR, PAIR)) < HEAD_DIM) == (_lane((PAIR, PAIR)) < HEAD_DIM)


def _bd(r2, dtype=None):
    r2 = r2.astype(dtype or BF16)
    left = _lane(r2.shape) < HEAD_DIM
    return jnp.concatenate([jnp.where(left, r2, 0.0), jnp.where(left, 0.0, r2)], axis=0)


def _pmm(l2, r2):
    return _dot(l2, _bd(r2))


def _pmm_nt(l2, r2):
    return _dot_nt(l2, _bd(r2))


def _pmm_tn(l2, r2):
    return jnp.where(_block_diag_mask(), _dot(l2.T, r2), 0.0)


def _segsum(x):
    ones = _block_diag_mask().astype(BF16)
    hi, lo = _split(x)
    return (jnp.dot(hi, ones, preferred_element_type=F32)
            + jnp.dot(lo, ones, preferred_element_type=F32))


def _expand(cols, ja, jb):
    left = _lane((cols.shape[0], PAIR)) < HEAD_DIM
    return jnp.where(left, cols[:, ja:ja + 1], cols[:, jb:jb + 1])


def _chunk_scan(x, combine, identity):
    pos = _row(x.shape) & (CHUNK - 1)
    step = 1
    while step < CHUNK:
        x = combine(x, jnp.where(pos >= step, pltpu.roll(x, step, axis=0), identity))
        step *= 2
    return x


def _causal_conv(z, carry_ref, w_ref):
    rows = z.shape[0]
    prev = carry_ref[...]
    carry_ref[...] = z[rows - CARRY_ROWS:, :]
    head_rows = _row(prev.shape)
    y = z * w_ref[CONV_WIDTH - 1:CONV_WIDTH, :]
    for k in range(1, CONV_WIDTH):
        zr = pltpu.roll(z, k, axis=0)
        pr = pltpu.roll(prev, k, axis=0)
        head = jnp.where(head_rows < k, pr, zr[:CARRY_ROWS])
        shifted = jnp.concatenate([head, zr[CARRY_ROWS:]], axis=0)
        y = y + shifted * w_ref[CONV_WIDTH - 1 - k:CONV_WIDTH - k, :]
    return y


def _modulated(x_ref, mod_ref):
    return (x_ref[0] * (1.0 + mod_ref[0, 1:2, :]) + mod_ref[0, 0:1, :]).astype(BF16)


def _pair_masks():
    row = _row((CHUNK, PAIR))
    col = _lane((CHUNK, PAIR)) & (HEAD_DIM - 1)
    return col <= row, col < row, col == row


def _row_form(col2, eye2):
    return jnp.sum(jnp.where(eye2, col2, 0.0), axis=0, keepdims=True)


def _ada_kernel(c_ref, w_ref, b_ref, o_ref):
    o_ref[0] = _dot(_silu(c_ref[...]), w_ref[0]) + b_ref[0]


def _ada_mod(c, ada_w, ada_b):
    depth, d, n = ada_w.shape
    rows = c.shape[0]
    b = -(-rows // 8) * 8
    c = jnp.pad(c, ((0, b - rows), (0, 0)))
    return pl.pallas_call(
        _ada_kernel,
        grid=(depth, n // ADA_COLS),
        in_specs=[
            pl.BlockSpec((b, d), lambda l, j: (0, 0)),
            pl.BlockSpec((1, d, ADA_COLS), lambda l, j: (l, 0, j)),
            pl.BlockSpec((1, 1, ADA_COLS), lambda l, j: (l, 0, j)),
        ],
        out_specs=pl.BlockSpec((1, b, ADA_COLS), lambda l, j: (l, 0, j)),
        out_shape=jax.ShapeDtypeStruct((depth, b, n), F32),
        compiler_params=pltpu.CompilerParams(vmem_limit_bytes=VMEM_LIMIT_BYTES),
        name="ada_mod",
    )(c, ada_w, ada_b.reshape(depth, 1, n))[:, :rows]


def _ffn_kernel(x_ref, mod_ref, w13_ref, w2_ref, g_ref, b_ref, o_ref):
    x = x_ref[0]
    h = _modulated(x_ref, mod_ref)
    acc = jnp.zeros(x.shape, F32)
    for c in range(D_FF // FFN_COLS):
        lo = c * FFN_COLS
        a = jnp.dot(h, w13_ref[:, lo:lo + FFN_COLS], preferred_element_type=F32)
        b = jnp.dot(h, w13_ref[:, D_FF + lo:D_FF + lo + FFN_COLS], preferred_element_type=F32)
        act = (_silu(a) * b).astype(BF16)
        acc = acc + jnp.dot(act, w2_ref[lo:lo + FFN_COLS, :], preferred_element_type=F32)
    y = ALPHA * x + (0.5 * (1.0 + mod_ref[0, 2:3, :])) * acc
    o_ref[0] = _layer_norm(y, g_ref[...], b_ref[...])


def _ffn(x, mod, w13, w2, ln_g, ln_b):
    b, t, d = x.shape
    const = lambda i, j: (0, 0)
    return pl.pallas_call(
        _ffn_kernel,
        grid=(b, t // FFN_ROWS),
        in_specs=[
            pl.BlockSpec((1, FFN_ROWS, d), lambda i, j: (i, j, 0)),
            pl.BlockSpec((1, 3, d), lambda i, j: (i, 0, 0)),
            pl.BlockSpec(w13.shape, const, pipeline_mode=pl.Buffered(1)),
            pl.BlockSpec(w2.shape, const, pipeline_mode=pl.Buffered(1)),
            pl.BlockSpec((1, d), const),
            pl.BlockSpec((1, d), const),
        ],
        out_specs=pl.BlockSpec((1, FFN_ROWS, d), lambda i, j: (i, j, 0)),
        out_shape=jax.ShapeDtypeStruct(x.shape, F32),
        compiler_params=pltpu.CompilerParams(vmem_limit_bytes=VMEM_LIMIT_BYTES),
        name="ffn",
    )(x, mod, w13, w2, ln_g.reshape(1, d), ln_b.reshape(1, d))


GDN_QKV = 3 * W_GDN
GDN_PROJ = GDN_QKV + W_GDN + SMALL


def _gdn_kernel(x_ref, mod_ref, w_ref, conv_ref, par_ref, norm_ref, o_ref, carry_ref, s_ref):
    @pl.when(pl.program_id(1) == 0)
    def _():
        carry_ref[...] = jnp.zeros_like(carry_ref)
        s_ref[...] = jnp.zeros_like(s_ref)

    proj = jnp.dot(_modulated(x_ref, mod_ref), w_ref[...], preferred_element_type=F32)
    qkv = _silu(_causal_conv(proj[:, :GDN_QKV], carry_ref, conv_ref))
    zs = proj[:, GDN_QKV + W_GDN:]
    g = -jnp.exp(par_ref[0:1, :]) * _softplus(zs + par_ref[1:2, :])
    beta = _sigmoid(zs)
    gcum = _chunk_scan(g, jnp.add, 0.0)
    incl2, strict2, eye2 = _pair_masks()
    eye_f = jnp.where(eye2, 1.0, 0.0)
    pairs = range(H_GDN // 2)
    chunks = range(MIX_ROWS // CHUNK)
    units = [(p, c) for p in pairs for c in chunks]

    q_n, k_n = [], []
    for p in pairs:
        q_all = qkv[:, p * PAIR:(p + 1) * PAIR]
        k_all = qkv[:, W_GDN + p * PAIR:W_GDN + (p + 1) * PAIR]
        q_n.append(q_all * lax.rsqrt(_segsum(q_all * q_all) + NORM_EPS) * (HEAD_DIM ** -0.5))
        k_n.append(k_all * lax.rsqrt(_segsum(k_all * k_all) + NORM_EPS))
    st = {}

    def prepare(p, c):
        rows = slice(c * CHUNK, (c + 1) * CHUNK)
        q2, k2 = q_n[p][rows], k_n[p][rows]
        v2 = qkv[rows, 2 * W_GDN + p * PAIR:2 * W_GDN + (p + 1) * PAIR]
        g2 = _expand(gcum[rows], 2 * p, 2 * p + 1)
        b2 = _expand(beta[rows], H_GDN + 2 * p, H_GDN + 2 * p + 1)
        g_last = g2[CHUNK - 1:CHUNK, :]
        decay = jnp.exp(jnp.where(incl2, g2 - _row_form(g2, eye2), -jnp.inf))
        exp_g = jnp.exp(g2)
        kb = k2 * b2
        gram = _dot_nt(jnp.concatenate([kb, q2], axis=0), _bd(k2))
        st[p, c] = dict(
            xp=-jnp.where(strict2, gram[:CHUNK] * decay, 0.0),
            attn=jnp.where(incl2, gram[CHUNK:] * decay, 0.0),
            rhs=jnp.concatenate([_bd(v2 * b2), _bd(kb * exp_g)], axis=1),
            q_dec=q2 * exp_g,
            k_dec=k2 * jnp.exp(g_last - g2),
            g_end=jnp.exp(g_last))

    def start_inverse(group):
        return {u: eye_f + st[u]["xp"] for u in group}

    def square(group):
        for u in group:
            st[u]["xp"] = _pmm(st[u]["xp"], st[u]["xp"])

    def fold(group, tinv):
        for u in group:
            tinv[u] = tinv[u] + _pmm(tinv[u], st[u]["xp"])

    def solve(group, tinv):
        for u in group:
            uw = _dot(tinv[u], st[u]["rhs"])
            st[u]["u"], st[u]["w"] = uw[:, :PAIR], uw[:, PAIR:]

    def mean_square(c):
        for p in pairs:
            st[p, c]["ms"] = _segsum(st[p, c]["o"] * st[p, c]["o"]) * (1.0 / HEAD_DIM)

    def emit(c):
        rows = slice(c * CHUNK, (c + 1) * CHUNK)
        for p in pairs:
            e = st[p, c]
            z2 = proj[rows, GDN_QKV + p * PAIR:GDN_QKV + (p + 1) * PAIR]
            o_ref[0, rows, p * PAIR:(p + 1) * PAIR] = (
                e["o"] * lax.rsqrt(e["ms"] + NORM_EPS) * norm_ref[...] * _silu(z2)).astype(o_ref.dtype)

    states = [s_ref[p] for p in pairs]

    def recur(c):
        ws_qs = [_dot(jnp.concatenate([st[p, c]["w"], st[p, c]["q_dec"]], axis=0), states[p])
                 for p in pairs]
        if c > 0:
            mean_square(c - 1)
        for p in pairs:
            e = st[p, c]
            v_new = e["u"] - ws_qs[p][:CHUNK]
            e["o"] = ws_qs[p][CHUNK:] + _pmm(e["attn"], v_new)
            states[p] = states[p] * e["g_end"] + _pmm_tn(e["k_dec"], v_new)
        if c > 0:
            emit(c - 1)

    half = len(chunks) // 2
    first = [(p, c) for c in chunks[:half] for p in pairs]
    second = [(p, c) for c in chunks[half:] for p in pairs]
    rounds = 5
    share = -(-len(second) // (rounds + 1))
    for u in first:
        prepare(*u)
    tinv = start_inverse(first)
    for r in range(rounds):
        square(first)
        for u in second[r * share:(r + 1) * share]:
            prepare(*u)
        fold(first, tinv)
    for u in second[rounds * share:]:
        prepare(*u)
    solve(first, tinv)
    tinv = start_inverse(second)
    for r in range(rounds):
        square(second)
        if r < half:
            recur(chunks[r])
        fold(second, tinv)
    solve(second, tinv)
    for c in chunks[min(rounds, half):]:
        recur(c)
    for p in pairs:
        s_ref[p] = states[p]
    mean_square(chunks[-1])
    emit(chunks[-1])


def _gdn(x, mod, w, conv, par, norm):
    b, t, d = x.shape
    const = lambda i, j: (0, 0)
    return pl.pallas_call(
        _gdn_kernel,
        grid=(b, t // MIX_ROWS),
        in_specs=[
            pl.BlockSpec((1, MIX_ROWS, d), lambda i, j: (i, j, 0)),
            pl.BlockSpec((1, 3, d), lambda i, j: (i, 0, 0)),
            pl.BlockSpec(w.shape, const),
            pl.BlockSpec(conv.shape, const),
            pl.BlockSpec(par.shape, const),
            pl.BlockSpec(norm.shape, const),
        ],
        out_specs=pl.BlockSpec((1, MIX_ROWS, W_GDN), lambda i, j: (i, j, 0)),
        out_shape=jax.ShapeDtypeStruct((b, t, W_GDN), BF16),
        scratch_shapes=[
            pltpu.VMEM((CARRY_ROWS, GDN_QKV), F32),
            pltpu.VMEM((H_GDN // 2, PAIR, PAIR), F32),
        ],
        compiler_params=pltpu.CompilerParams(vmem_limit_bytes=VMEM_LIMIT_BYTES),
        name="gdn",
    )(x, mod, w, conv, par, norm)


ML_QK = 2 * W_MLSTM
ML_PROJ = ML_QK + 2 * W_MLSTM + SMALL


def _mlstm_kernel(x_ref, mod_ref, w_ref, conv_ref, par_ref, norm_ref, o_ref,
                  carry_ref, c_ref, n_ref, m_ref):
    @pl.when(pl.program_id(1) == 0)
    def _():
        carry_ref[...] = jnp.zeros_like(carry_ref)
        c_ref[...] = jnp.zeros_like(c_ref)
        n_ref[...] = jnp.zeros_like(n_ref)
        m_ref[...] = jnp.zeros_like(m_ref)

    proj = jnp.dot(_modulated(x_ref, mod_ref), w_ref[...], preferred_element_type=F32)
    qk = _silu(_causal_conv(proj[:, :ML_QK], carry_ref, conv_ref))
    zs = proj[:, ML_QK + 2 * W_MLSTM:] + par_ref[0:1, :]
    i_pre = zs
    logf = -_softplus(-zs)
    bcum = _chunk_scan(logf, jnp.add, 0.0)
    bcum_i = pltpu.roll(bcum, SMALL - H_MLSTM, axis=1)
    i_minus_b = i_pre - bcum_i
    run_max = _chunk_scan(i_minus_b, jnp.maximum, -jnp.inf)
    incl2, _, eye2 = _pair_masks()
    ones_bd = _block_diag_mask().astype(BF16)
    pairs = range(H_MLSTM // 2)
    chunks = range(MIX_ROWS // CHUNK)
    units = [(p, c) for p in pairs for c in chunks]

    def operands(p, c):
        rows = slice(c * CHUNK, (c + 1) * CHUNK)
        q2 = qk[rows, p * PAIR:(p + 1) * PAIR]
        k2 = qk[rows, W_MLSTM + p * PAIR:W_MLSTM + (p + 1) * PAIR] * (HEAD_DIM ** -0.5)
        v2 = proj[rows, ML_QK + p * PAIR:ML_QK + (p + 1) * PAIR]
        return q2, k2, v2

    st = {}
    for p in pairs:
        m_state = m_ref[p]
        for c in chunks:
            rows = slice(c * CHUNK, (c + 1) * CHUNK)
            b2 = _expand(bcum_i[rows], 2 * p, 2 * p + 1)
            i2 = _expand(i_pre[rows], 2 * p, 2 * p + 1)
            d_max = b2 + _expand(run_max[rows], 2 * p, 2 * p + 1)
            b_last = b2[CHUNK - 1:CHUNK, :]
            w_end = b_last - b2 + i2
            inter = b2 + m_state
            m_t = jnp.maximum(inter, d_max)
            inter_end = b_last + m_state
            m_new = jnp.maximum(inter_end, jnp.max(w_end, axis=0, keepdims=True))
            d2 = jnp.where(incl2, b2 + _row_form(i2 - b2, eye2), -jnp.inf)
            st[p, c] = dict(m_t=m_t, a_inter=jnp.exp(inter - m_t), decay=jnp.exp(d2 - m_t),
                            s=jnp.exp(w_end - m_new), dec=jnp.exp(inter_end - m_new))
            m_state = m_new
        m_ref[p] = m_state

    for p, c in units:
        q2, k2, v2 = operands(p, c)
        e = st[p, c]
        e["p2"] = _pmm_nt(q2, k2) * e["decay"]
        sk = e["s"] * k2
        e["kv"] = _pmm_tn(sk, v2)
        e["ksum"] = jnp.sum(sk, axis=0, keepdims=True)
    for p, c in units:
        st[p, c]["intra"] = _pmm(st[p, c]["p2"], operands(p, c)[2])

    for p in pairs:
        c_state, n_state = c_ref[p], n_ref[p]
        for c in chunks:
            e = st[p, c]
            e["c_in"], e["n_in"] = c_state, n_state
            c_state = e["dec"] * c_state + e["kv"]
            n_state = e["dec"] * n_state + e["ksum"]
        c_ref[p], n_ref[p] = c_state, n_state

    for p, c in units:
        q2 = operands(p, c)[0]
        e = st[p, c]
        e["num"] = e["a_inter"] * _dot(q2, e["c_in"]) + e["intra"]
        den_hi, den_lo = _split(e["a_inter"] * q2 * e["n_in"] + e["p2"])
        e["den"] = (jnp.dot(den_hi, ones_bd, preferred_element_type=F32)
                    + jnp.dot(den_lo, ones_bd, preferred_element_type=F32))
    for p, c in units:
        rows = slice(c * CHUNK, (c + 1) * CHUNK)
        e = st[p, c]
        hid = e["num"] / jnp.maximum(jnp.abs(e["den"]), jnp.exp(-e["m_t"]))
        hid = hid * _sigmoid(proj[rows, ML_QK + W_MLSTM + p * PAIR:ML_QK + W_MLSTM + (p + 1) * PAIR])
        e["cen"] = hid - _segsum(hid) * (1.0 / HEAD_DIM)
    for p, c in units:
        e = st[p, c]
        e["var"] = _segsum(e["cen"] * e["cen"]) * (1.0 / HEAD_DIM)
    for p, c in units:
        rows = slice(c * CHUNK, (c + 1) * CHUNK)
        lanes = slice(p * PAIR, (p + 1) * PAIR)
        e = st[p, c]
        o_ref[0, rows, lanes] = (e["cen"] * lax.rsqrt(e["var"] + NORM_EPS)
                                 * norm_ref[:, lanes]).astype(o_ref.dtype)


def _mlstm(x, mod, w, conv, par, norm):
    b, t, d = x.shape
    const = lambda i, j: (0, 0)
    return pl.pallas_call(
        _mlstm_kernel,
        grid=(b, t // MIX_ROWS),
        in_specs=[
            pl.BlockSpec((1, MIX_ROWS, d), lambda i, j: (i, j, 0)),
            pl.BlockSpec((1, 3, d), lambda i, j: (i, 0, 0)),
            pl.BlockSpec(w.shape, const),
            pl.BlockSpec(conv.shape, const),
            pl.BlockSpec(par.shape, const),
            pl.BlockSpec(norm.shape, const),
        ],
        out_specs=pl.BlockSpec((1, MIX_ROWS, W_MLSTM), lambda i, j: (i, j, 0)),
        out_shape=jax.ShapeDtypeStruct((b, t, W_MLSTM), BF16),
        scratch_shapes=[
            pltpu.VMEM((CARRY_ROWS, ML_QK), F32),
            pltpu.VMEM((H_MLSTM // 2, PAIR, PAIR), F32),
            pltpu.VMEM((H_MLSTM // 2, 1, PAIR), F32),
            pltpu.VMEM((H_MLSTM // 2, 1, PAIR), F32),
        ],
        compiler_params=pltpu.CompilerParams(vmem_limit_bytes=VMEM_LIMIT_BYTES),
        name="mlstm",
    )(x, mod, w, conv, par, norm)


def _moba_kernel(x_ref, mod_ref, w_ref, o_ref, k_ref, vt_ref, kmean_ref,
                 s0_ref, s1_ref, e0_ref, e1_ref, acc_ref):
    t = pl.program_id(1)
    n_blk = kmean_ref.shape[0]

    @pl.when(t == 0)
    def _():
        kmean_ref[...] = jnp.zeros_like(kmean_ref)

    proj = jnp.dot(_modulated(x_ref, mod_ref), w_ref[...], preferred_element_type=F32)
    q = proj[:, :W_MOBA]
    k = proj[:, W_MOBA:2 * W_MOBA]
    v = proj[:, 2 * W_MOBA:]
    k_ref[t] = k.astype(BF16)
    ones_rows = jnp.where(_row((VT_ROWS - PAIR, MOBA_BLOCK)) == 0, 1.0, 0.0)
    for p in range(H_MOBA // 2):
        vt_ref[t, p * VT_ROWS:(p + 1) * VT_ROWS, :] = jnp.concatenate(
            [v[:, p * PAIR:(p + 1) * PAIR].T, ones_rows], axis=0).astype(BF16)
    k_mean = jnp.mean(k, axis=0, keepdims=True)
    kmean_ref[...] = jnp.where(_row(kmean_ref.shape) == t, k_mean, kmean_ref[...])
    kmean = kmean_ref[...]

    n_q = 2 * MOBA_BLOCK
    blk = _row((n_blk, n_q)).astype(F32)
    t_f = t.astype(F32)
    causal = _row((MOBA_BLOCK, n_q)) <= (_lane((MOBA_BLOCK, n_q)) & (MOBA_BLOCK - 1))
    left = _lane((MOBA_BLOCK, PAIR)) < HEAD_DIM

    pairs = range(H_MOBA // 2)
    s_bufs, e_bufs = (s0_ref, s1_ref), (e0_ref, e1_ref)
    qs_all, picks_all, init = [], [], []
    for p in pairs:
        lanes = slice(p * PAIR, (p + 1) * PAIR)
        q_bd = _bd(q[:, lanes], F32)
        q_hi, q_lo = _split(q_bd)
        m_hi, m_lo = _split(kmean[:, lanes])
        gate = _dot_nt(m_hi, q_hi) + _dot_nt(m_hi, q_lo) + _dot_nt(m_lo, q_hi)
        gate = jnp.where(blk < t_f, gate, -jnp.inf)
        picks = []
        for r in range(MOBA_TOPK):
            best = jnp.max(gate, axis=0, keepdims=True)
            idx = jnp.min(jnp.where(gate == best, blk, float(n_blk)), axis=0, keepdims=True)
            gate = jnp.where(blk == idx, -jnp.inf, gate)
            picks.append(jnp.where(r < t, idx, -1.0))
        qs = (q_bd * (HEAD_DIM ** -0.5 * LOG2_E)).astype(BF16)
        s = jnp.where(causal, _dot_nt(k[:, lanes], qs), -jnp.inf)
        m_run = jnp.max(s, axis=0, keepdims=True)
        e = jnp.exp2(s - m_run)
        acc_ref[p] = _dot(vt_ref[t, p * VT_ROWS:(p + 1) * VT_ROWS, :], e)
        s_bufs[0][p] = _dot_nt(k_ref[0, :, lanes], qs)
        e_bufs[1][p] = jnp.zeros((MOBA_BLOCK, n_q), BF16)
        init.append((m_run, jnp.ones((1, n_q), F32)))
        qs_all.append(qs)
        picks_all.append(picks)

    def step(i, p, state, cur, nxt):
        m_run, scale_prev = state
        rows = slice(p * PAIR, (p + 1) * PAIR)
        vt_rows = slice(p * VT_ROWS, (p + 1) * VT_ROWS)
        s_bufs[nxt][p] = _dot_nt(k_ref[jnp.minimum(i + 1, t), :, rows], qs_all[p])
        acc_ref[p] = acc_ref[p] * scale_prev + _dot(vt_ref[jnp.maximum(i - 1, 0), vt_rows, :], e_bufs[nxt][p])
        i_f = i.astype(F32)
        chosen = (picks_all[p][0] == i_f) | (picks_all[p][1] == i_f) | (picks_all[p][2] == i_f)
        bias = jnp.where(chosen, 0.0, -jnp.inf)
        m_new = jnp.maximum(m_run, jnp.max(s_bufs[cur][p], axis=0, keepdims=True) + bias)
        e_bufs[cur][p] = jnp.exp2(s_bufs[cur][p] + (bias - m_new)).astype(BF16)
        return m_new, jnp.exp2(m_run - m_new)

    def body(i2, carry):
        carry = tuple(step(2 * i2, p, carry[p], 0, 1) for p in pairs)
        return tuple(step(2 * i2 + 1, p, carry[p], 1, 0) for p in pairs)

    trips = (t + 1) // 2
    final = lax.fori_loop(0, trips, body, tuple(init))
    last = jnp.minimum(jnp.maximum(2 * trips - 1, 0), t)
    for p in pairs:
        _, scale_prev = final[p]
        vt_rows = slice(p * VT_ROWS, (p + 1) * VT_ROWS)
        acc = acc_ref[p] * scale_prev + _dot(vt_ref[last, vt_rows, :], e_bufs[1][p])
        out = (acc[:PAIR] / acc[PAIR:PAIR + 1]).T
        o_ref[0, :, p * PAIR:(p + 1) * PAIR] = jnp.where(
            left, out[:MOBA_BLOCK], out[MOBA_BLOCK:]).astype(o_ref.dtype)


def _moba(x, mod, w):
    b, t, d = x.shape
    n_blk = t // MOBA_BLOCK
    assert n_blk >= MOBA_TOPK
    const = lambda i, j: (0, 0)
    return pl.pallas_call(
        _moba_kernel,
        grid=(b, n_blk),
        in_specs=[
            pl.BlockSpec((1, MOBA_BLOCK, d), lambda i, j: (i, j, 0)),
            pl.BlockSpec((1, 3, d), lambda i, j: (i, 0, 0)),
            pl.BlockSpec(w.shape, const),
        ],
        out_specs=pl.BlockSpec((1, MOBA_BLOCK, W_MOBA), lambda i, j: (i, j, 0)),
        out_shape=jax.ShapeDtypeStruct((b, t, W_MOBA), BF16),
        scratch_shapes=[
            pltpu.VMEM((n_blk, MOBA_BLOCK, W_MOBA), BF16),
            pltpu.VMEM((n_blk, (H_MOBA // 2) * VT_ROWS, MOBA_BLOCK), BF16),
            pltpu.VMEM((n_blk, W_MOBA), F32),
            pltpu.VMEM((H_MOBA // 2, MOBA_BLOCK, 2 * MOBA_BLOCK), F32),
            pltpu.VMEM((H_MOBA // 2, MOBA_BLOCK, 2 * MOBA_BLOCK), F32),
            pltpu.VMEM((H_MOBA // 2, MOBA_BLOCK, 2 * MOBA_BLOCK), BF16),
            pltpu.VMEM((H_MOBA // 2, MOBA_BLOCK, 2 * MOBA_BLOCK), BF16),
            pltpu.VMEM((H_MOBA // 2, VT_ROWS, 2 * MOBA_BLOCK), F32),
        ],
        compiler_params=pltpu.CompilerParams(vmem_limit_bytes=VMEM_LIMIT_BYTES),
        name="moba",
    )(x, mod, w)


def _mix_out_kernel(x_ref, mod_ref, oa_ref, ob_ref, oc_ref, w_ref, g_ref, b_ref, o_ref):
    y = jnp.dot(oa_ref[0], w_ref[:W_GDN, :], preferred_element_type=F32)
    y = y + jnp.dot(ob_ref[0], w_ref[W_GDN:W_GDN + W_MOBA, :], preferred_element_type=F32)
    y = y + jnp.dot(oc_ref[0], w_ref[W_GDN + W_MOBA:, :], preferred_element_type=F32)
    v = ALPHA * x_ref[0] + (1.0 + mod_ref[0, 2:3, :]) * y
    o_ref[0] = _layer_norm(v, g_ref[...], b_ref[...])


def _mix_out(x, mod, oa, ob, oc, w_out, ln_g, ln_b):
    b, t, d = x.shape
    const = lambda i, j: (0, 0)
    tile = lambda width: pl.BlockSpec((1, FFN_ROWS, width), lambda i, j: (i, j, 0))
    return pl.pallas_call(
        _mix_out_kernel,
        grid=(b, t // FFN_ROWS),
        in_specs=[
            tile(d),
            pl.BlockSpec((1, 3, d), lambda i, j: (i, 0, 0)),
            tile(W_GDN), tile(W_MOBA), tile(W_MLSTM),
            pl.BlockSpec(w_out.shape, const),
            pl.BlockSpec((1, d), const),
            pl.BlockSpec((1, d), const),
        ],
        out_specs=tile(d),
        out_shape=jax.ShapeDtypeStruct(x.shape, F32),
        compiler_params=pltpu.CompilerParams(vmem_limit_bytes=VMEM_LIMIT_BYTES),
        name="mix_out",
    )(x, mod, oa, ob, oc, w_out, ln_g.reshape(1, d), ln_b.reshape(1, d))


def _small_cols(*cols):
    cat = jnp.concatenate(cols, axis=-1)
    return jnp.pad(cat, [(0, 0)] * (cat.ndim - 1) + [(0, SMALL - cat.shape[-1])])


def _split_w_in(w_in):
    sizes = [3 * W_GDN, H_GDN, H_GDN, W_GDN, 3 * W_MOBA,
             2 * W_MLSTM, W_MLSTM, H_MLSTM, H_MLSTM, W_MLSTM]
    parts, lo = [], 0
    for s in sizes:
        parts.append(w_in[:, lo:lo + s])
        lo += s
    qkv_a, a_a, b_a, z_a, qkv_b, qk_c, v_c, i_c, f_c, o_c = parts
    w_gdn = jnp.concatenate([qkv_a, z_a, _small_cols(a_a, b_a)], axis=1).astype(BF16)
    w_moba = qkv_b.astype(BF16)
    w_ml = jnp.concatenate([qk_c, v_c, o_c, _small_cols(i_c, f_c)], axis=1).astype(BF16)
    return w_gdn, w_moba, w_ml


def kernel(x, c, ada_w, ada_b, ffn_w13, ffn_w2, w_in, w_out, gdn_conv, gdn_a_log, gdn_dt_bias, gdn_norm,
           mlstm_conv, mlstm_i_bias, mlstm_f_bias, mlstm_norm, ln_g, ln_b):
    b = x.shape[0]
    depth = ada_w.shape[0]
    mod_all = _ada_mod(c, ada_w, ada_b).reshape(depth, b, N_SUB, 3, D_MODEL)
    for l in range(depth):
        mod = mod_all[l]
        w_gdn, w_moba, w_ml = _split_w_in(w_in[l])
        gdn_par = jnp.stack([_small_cols(gdn_a_log[l]), _small_cols(gdn_dt_bias[l])])
        gdn_par = jnp.pad(gdn_par, ((0, 6), (0, 0)))
        ml_par = jnp.pad(_small_cols(mlstm_i_bias[l], mlstm_f_bias[l])[None, :], ((0, 7), (0, 0)))
        gdn_norm2 = jnp.tile(gdn_norm[l], 2).reshape(1, PAIR)

        x = _ffn(x, mod[:, 0], ffn_w13[l, 0].astype(BF16), ffn_w2[l, 0].astype(BF16), ln_g[l, 0], ln_b[l, 0])
        oa = _gdn(x, mod[:, 1], w_gdn, gdn_conv[l], gdn_par, gdn_norm2)
        ob = _moba(x, mod[:, 1], w_moba)
        oc = _mlstm(x, mod[:, 1], w_ml, mlstm_conv[l], ml_par, mlstm_norm[l].reshape(1, W_MLSTM))
        x = _mix_out(x, mod[:, 1], oa, ob, oc, w_out[l].astype(BF16), ln_g[l, 1], ln_b[l, 1])
        x = _ffn(x, mod[:, 2], ffn_w13[l, 1].astype(BF16), ffn_w2[l, 1].astype(BF16), ln_g[l, 2], ln_b[l, 2])
    return x
```

```python
import functools

import jax
import jax.numpy as jnp
from jax import lax
from jax.experimental import pallas as pl
from jax.experimental.pallas import tpu as pltpu

F32 = jnp.float32
BF16 = jnp.bfloat16

D_MODEL = 1024
DEPTH = 2
HEAD_DIM = 64
PAIR = 2 * HEAD_DIM
H_GDN = 6
H_MOBA = 4
H_MLSTM = 6
W_GDN = H_GDN * HEAD_DIM
W_MOBA = H_MOBA * HEAD_DIM
W_MLSTM = H_MLSTM * HEAD_DIM
CONV_WIDTH = 4
CHUNK = 64
MOBA_BLOCK = 256
MOBA_TOPK = 3
VT_ROWS = PAIR + 16
D_FF = 2816
N_SUB = 3
ALPHA = (2 * DEPTH) ** 0.25
LN_EPS = 1e-5
NORM_EPS = 1e-6
LOG2_E = 1.4426950408889634
SMALL = 128
CARRY_ROWS = 8

VMEM_LIMIT_BYTES = 56 * 1024 * 1024
FFN_ROWS = 512
FFN_COLS = 256
MIX_ROWS = 512
ADA_COLS = 1152


def _sigmoid(x):
    return 1.0 / (1.0 + jnp.exp(-x))


def _silu(x):
    return x * _sigmoid(x)


def _softplus(x):
    return jnp.maximum(x, 0.0) + jnp.log1p(jnp.exp(-jnp.abs(x)))


def _layer_norm(v, g, b):
    mu = jnp.mean(v, -1, keepdims=True)
    d = v - mu
    var = jnp.mean(d * d, -1, keepdims=True)
    return d * lax.rsqrt(var + LN_EPS) * g + b


def _dot(a, b):
    return jnp.dot(a.astype(BF16), b.astype(BF16), preferred_element_type=F32)


def _dot_nt(a, b):
    return lax.dot_general(a.astype(BF16), b.astype(BF16), (((1,), (1,)), ((), ())),
                           preferred_element_type=F32)


def _split(x):
    hi = x.astype(BF16)
    lo = (x - hi.astype(F32)).astype(BF16)
    return hi, lo


def _lane(shape):
    return lax.broadcasted_iota(jnp.int32, shape, len(shape) - 1)


def _row(shape):
    return lax.broadcasted_iota(jnp.int32, shape, len(shape) - 2)


def _block_diag_mask():
    return (_row((PAIR, PAIR)) < HEAD_DIM) == (_lane((PAIR, PAIR)) < HEAD_DIM)


def _bd(r2, dtype=None):
    r2 = r2.astype(dtype or BF16)
    left = _lane(r2.shape) < HEAD_DIM
    return jnp.concatenate([jnp.where(left, r2, 0.0), jnp.where(left, 0.0, r2)], axis=0)


def _pmm(l2, r2):
    return _dot(l2, _bd(r2))


def _pmm_nt(l2, r2):
    return _dot_nt(l2, _bd(r2))


def _segsums(xs):
    ones = _block_diag_mask().astype(BF16)
    parts = [part for x in xs for part in _split(x)]
    sums = jnp.dot(jnp.concatenate(parts, axis=0), ones, preferred_element_type=F32)
    out, lo = [], 0
    for x in xs:
        n = x.shape[0]
        out.append(sums[lo:lo + n] + sums[lo + n:lo + 2 * n])
        lo += 2 * n
    return out


def _segsum(x):
    return _segsums([x])[0]


def _expand(cols, ja, jb):
    left = _lane((cols.shape[0], PAIR)) < HEAD_DIM
    return jnp.where(left, cols[:, ja:ja + 1], cols[:, jb:jb + 1])


def _chunk_scan(x, combine, identity):
    pos = _row(x.shape) & (CHUNK - 1)
    step = 1
    while step < CHUNK:
        x = combine(x, jnp.where(pos >= step, pltpu.roll(x, step, axis=0), identity))
        step *= 2
    return x


def _causal_conv(z, carry_ref, w_ref):
    rows = z.shape[0]
    prev = carry_ref[...]
    carry_ref[...] = z[rows - CARRY_ROWS:, :]
    head_rows = _row(prev.shape)
    y = z * w_ref[CONV_WIDTH - 1:CONV_WIDTH, :]
    for k in range(1, CONV_WIDTH):
        zr = pltpu.roll(z, k, axis=0)
        pr = pltpu.roll(prev, k, axis=0)
        head = jnp.where(head_rows < k, pr, zr[:CARRY_ROWS])
        shifted = jnp.concatenate([head, zr[CARRY_ROWS:]], axis=0)
        y = y + shifted * w_ref[CONV_WIDTH - 1 - k:CONV_WIDTH - k, :]
    return y


def _modulated(x_ref, mod_ref):
    return (x_ref[0] * (1.0 + mod_ref[0, 1:2, :]) + mod_ref[0, 0:1, :]).astype(BF16)


def _pair_masks():
    row = _row((CHUNK, PAIR))
    col = _lane((CHUNK, PAIR)) & (HEAD_DIM - 1)
    return col <= row, col < row, col == row


def _row_form(col2, eye2):
    return jnp.sum(jnp.where(eye2, col2, 0.0), axis=0, keepdims=True)


def _ada_kernel(c_ref, w_ref, b_ref, o_ref):
    o_ref[0] = _dot(_silu(c_ref[...]), w_ref[0]) + b_ref[0]


def _ada_mod(c, ada_w, ada_b):
    depth, d, n = ada_w.shape
    rows = c.shape[0]
    b = -(-rows // 8) * 8
    c = jnp.pad(c, ((0, b - rows), (0, 0)))
    return pl.pallas_call(
        _ada_kernel,
        grid=(depth, n // ADA_COLS),
        in_specs=[
            pl.BlockSpec((b, d), lambda l, j: (0, 0)),
            pl.BlockSpec((1, d, ADA_COLS), lambda l, j: (l, 0, j)),
            pl.BlockSpec((1, 1, ADA_COLS), lambda l, j: (l, 0, j)),
        ],
        out_specs=pl.BlockSpec((1, b, ADA_COLS), lambda l, j: (l, 0, j)),
        out_shape=jax.ShapeDtypeStruct((depth, b, n), F32),
        compiler_params=pltpu.CompilerParams(vmem_limit_bytes=VMEM_LIMIT_BYTES),
        name="ada_mod",
    )(c, ada_w, ada_b.reshape(depth, 1, n))[:, :rows]


def _swiglu_sublayer(x, mod_ref, w13_ref, w2_ref, g_ref, b_ref):
    h = (x * (1.0 + mod_ref[0, 1:2, :]) + mod_ref[0, 0:1, :]).astype(BF16)
    acc = jnp.zeros(x.shape, F32)
    for c in range(D_FF // FFN_COLS):
        lo = c * FFN_COLS
        a = jnp.dot(h, w13_ref[:, lo:lo + FFN_COLS], preferred_element_type=F32)
        b = jnp.dot(h, w13_ref[:, D_FF + lo:D_FF + lo + FFN_COLS], preferred_element_type=F32)
        act = (_silu(a) * b).astype(BF16)
        acc = acc + jnp.dot(act, w2_ref[lo:lo + FFN_COLS, :], preferred_element_type=F32)
    y = ALPHA * x + (0.5 * (1.0 + mod_ref[0, 2:3, :])) * acc
    return _layer_norm(y, g_ref[...], b_ref[...])


def _ffn_kernel(x_ref, mod_ref, w13_ref, w2_ref, g_ref, b_ref, o_ref):
    o_ref[0] = _swiglu_sublayer(x_ref[0], mod_ref, w13_ref, w2_ref, g_ref, b_ref)


def _ffn(x, mod, w13, w2, ln_g, ln_b):
    b, t, d = x.shape
    const = lambda i, j: (0, 0)
    return pl.pallas_call(
        _ffn_kernel,
        grid=(b, t // FFN_ROWS),
        in_specs=[
            pl.BlockSpec((1, FFN_ROWS, d), lambda i, j: (i, j, 0)),
            pl.BlockSpec((1, 3, d), lambda i, j: (i, 0, 0)),
            pl.BlockSpec(w13.shape, const, pipeline_mode=pl.Buffered(1)),
            pl.BlockSpec(w2.shape, const, pipeline_mode=pl.Buffered(1)),
            pl.BlockSpec((1, d), const),
            pl.BlockSpec((1, d), const),
        ],
        out_specs=pl.BlockSpec((1, FFN_ROWS, d), lambda i, j: (i, j, 0)),
        out_shape=jax.ShapeDtypeStruct(x.shape, F32),
        compiler_params=pltpu.CompilerParams(vmem_limit_bytes=VMEM_LIMIT_BYTES),
        name="ffn",
    )(x, mod, w13, w2, ln_g.reshape(1, d), ln_b.reshape(1, d))


GDN_QKV = 3 * W_GDN
GDN_PROJ = GDN_QKV + W_GDN + SMALL


def _gdn_kernel(x_ref, mod_ref, w_ref, conv_ref, par_ref, norm_ref, o_ref, carry_ref, s_ref):
    @pl.when(pl.program_id(1) == 0)
    def _():
        carry_ref[...] = jnp.zeros_like(carry_ref)
        s_ref[...] = jnp.zeros_like(s_ref)

    proj = jnp.dot(_modulated(x_ref, mod_ref), w_ref[...], preferred_element_type=F32)
    qkv = _silu(_causal_conv(proj[:, :GDN_QKV], carry_ref, conv_ref))
    zs = proj[:, GDN_QKV + W_GDN:]
    g = -jnp.exp(par_ref[0:1, :]) * _softplus(zs + par_ref[1:2, :])
    beta = _sigmoid(zs)
    gcum = _chunk_scan(g, jnp.add, 0.0)
    incl2, strict2, eye2 = _pair_masks()
    eye_f = jnp.where(eye2, 1.0, 0.0)
    pairs = range(H_GDN // 2)
    chunks = range(MIX_ROWS // CHUNK)
    units = [(p, c) for p in pairs for c in chunks]

    q_n, k_n = [], []
    for p in pairs:
        q_all = qkv[:, p * PAIR:(p + 1) * PAIR]
        k_all = qkv[:, W_GDN + p * PAIR:W_GDN + (p + 1) * PAIR]
        q_n.append(q_all * lax.rsqrt(_segsum(q_all * q_all) + NORM_EPS) * (HEAD_DIM ** -0.5))
        k_n.append(k_all * lax.rsqrt(_segsum(k_all * k_all) + NORM_EPS))
    st = {}

    def prepare(p, c):
        rows = slice(c * CHUNK, (c + 1) * CHUNK)
        q2, k2 = q_n[p][rows], k_n[p][rows]
        v2 = qkv[rows, 2 * W_GDN + p * PAIR:2 * W_GDN + (p + 1) * PAIR]
        g2 = _expand(gcum[rows], 2 * p, 2 * p + 1)
        b2 = _expand(beta[rows], H_GDN + 2 * p, H_GDN + 2 * p + 1)
        g_last = g2[CHUNK - 1:CHUNK, :]
        decay = jnp.exp(jnp.where(incl2, g2 - _row_form(g2, eye2), -jnp.inf))
        exp_g = jnp.exp(g2)
        kb = k2 * b2
        gram = _dot_nt(jnp.concatenate([kb, q2], axis=0), _bd(k2))
        st[p, c] = dict(
            xp=-jnp.where(strict2, gram[:CHUNK] * decay, 0.0),
            attn=jnp.where(incl2, gram[CHUNK:] * decay, 0.0),
            rhs=jnp.concatenate([_bd(v2 * b2), _bd(kb * exp_g)], axis=1),
            q_dec=q2 * exp_g,
            k_dec_t=_bd(k2 * jnp.exp(g_last - g2), F32).T,
            g_end=jnp.exp(g_last))

    def inverse_rounds(group):
        tinv = {u: eye_f + st[u]["xp"] for u in group}

        def square():
            for u in group:
                st[u]["xp"] = _pmm(st[u]["xp"], st[u]["xp"])

        def fold_and_square():
            for u in group:
                x = st[u]["xp"]
                both = _pmm(jnp.concatenate([tinv[u], x], axis=0), x)
                tinv[u] = tinv[u] + both[:CHUNK]
                st[u]["xp"] = both[CHUNK:]

        def fold():
            for u in group:
                tinv[u] = tinv[u] + _pmm(tinv[u], st[u]["xp"])

        return tinv, [square] + [fold_and_square] * 4 + [fold]

    def solve(group, tinv):
        for u in group:
            uw = _dot(tinv[u], st[u]["rhs"])
            st[u]["u"], st[u]["w"] = uw[:, :PAIR], uw[:, PAIR:]

    def mean_square(c):
        sums = _segsums([st[p, c]["o"] * st[p, c]["o"] for p in pairs])
        for p in pairs:
            st[p, c]["ms"] = sums[p] * (1.0 / HEAD_DIM)

    def emit(c):
        rows = slice(c * CHUNK, (c + 1) * CHUNK)
        for p in pairs:
            e = st[p, c]
            z2 = proj[rows, GDN_QKV + p * PAIR:GDN_QKV + (p + 1) * PAIR]
            o_ref[0, rows, p * PAIR:(p + 1) * PAIR] = (
                e["o"] * lax.rsqrt(e["ms"] + NORM_EPS) * norm_ref[...] * _silu(z2)).astype(o_ref.dtype)

    states = [s_ref[p] for p in pairs]

    def recur(c):
        ws_qs = [_dot(jnp.concatenate([st[p, c]["w"], st[p, c]["q_dec"]], axis=0), states[p])
                 for p in pairs]
        if c > 0:
            mean_square(c - 1)
        for p in pairs:
            e = st[p, c]
            v_new = e["u"] - ws_qs[p][:CHUNK]
            both = _dot(jnp.concatenate([e["attn"], e["k_dec_t"]], axis=0), _bd(v_new))
            e["o"] = ws_qs[p][CHUNK:] + both[:CHUNK]
            states[p] = states[p] * e["g_end"] + both[CHUNK:]
        if c > 0:
            emit(c - 1)

    half = len(chunks) // 2
    first = [(p, c) for c in chunks[:half] for p in pairs]
    second = [(p, c) for c in chunks[half:] for p in pairs]
    for u in first:
        prepare(*u)
    tinv, rounds = inverse_rounds(first)
    share = -(-len(second) // len(rounds))
    for r, run in enumerate(rounds):
        run()
        for u in second[r * share:(r + 1) * share]:
            prepare(*u)
    solve(first, tinv)
    tinv, rounds = inverse_rounds(second)
    for r, run in enumerate(rounds):
        run()
        if r < half:
            recur(chunks[r])
    solve(second, tinv)
    for c in chunks[min(len(rounds), half):]:
        recur(c)
    for p in pairs:
        s_ref[p] = states[p]
    mean_square(chunks[-1])
    emit(chunks[-1])


def _gdn(x, mod, w, conv, par, norm):
    b, t, d = x.shape
    const = lambda i, j: (0, 0)
    return pl.pallas_call(
        _gdn_kernel,
        grid=(b, t // MIX_ROWS),
        in_specs=[
            pl.BlockSpec((1, MIX_ROWS, d), lambda i, j: (i, j, 0)),
            pl.BlockSpec((1, 3, d), lambda i, j: (i, 0, 0)),
            pl.BlockSpec(w.shape, const),
            pl.BlockSpec(conv.shape, const),
            pl.BlockSpec(par.shape, const),
            pl.BlockSpec(norm.shape, const),
        ],
        out_specs=pl.BlockSpec((1, MIX_ROWS, W_GDN), lambda i, j: (i, j, 0)),
        out_shape=jax.ShapeDtypeStruct((b, t, W_GDN), BF16),
        scratch_shapes=[
            pltpu.VMEM((CARRY_ROWS, GDN_QKV), F32),
            pltpu.VMEM((H_GDN // 2, PAIR, PAIR), F32),
        ],
        compiler_params=pltpu.CompilerParams(vmem_limit_bytes=VMEM_LIMIT_BYTES),
        name="gdn",
    )(x, mod, w, conv, par, norm)


ML_QK = 2 * W_MLSTM
ML_PROJ = ML_QK + 2 * W_MLSTM + SMALL


def _mlstm_kernel(x_ref, mod_ref, w_ref, conv_ref, par_ref, norm_ref, o_ref,
                  carry_ref, c_ref, n_ref, m_ref):
    @pl.when(pl.program_id(1) == 0)
    def _():
        carry_ref[...] = jnp.zeros_like(carry_ref)
        c_ref[...] = jnp.zeros_like(c_ref)
        n_ref[...] = jnp.zeros_like(n_ref)
        m_ref[...] = jnp.zeros_like(m_ref)

    proj = jnp.dot(_modulated(x_ref, mod_ref), w_ref[...], preferred_element_type=F32)
    qk = _silu(_causal_conv(proj[:, :ML_QK], carry_ref, conv_ref))
    zs = proj[:, ML_QK + 2 * W_MLSTM:] + par_ref[0:1, :]
    i_pre = zs
    logf = -_softplus(-zs)
    bcum = _chunk_scan(logf, jnp.add, 0.0)
    bcum_i = pltpu.roll(bcum, SMALL - H_MLSTM, axis=1)
    i_minus_b = i_pre - bcum_i
    run_max = _chunk_scan(i_minus_b, jnp.maximum, -jnp.inf)
    incl2, _, eye2 = _pair_masks()
    pairs = range(H_MLSTM // 2)
    chunks = range(MIX_ROWS // CHUNK)
    units = [(p, c) for p in pairs for c in chunks]

    def operands(p, c):
        rows = slice(c * CHUNK, (c + 1) * CHUNK)
        q2 = qk[rows, p * PAIR:(p + 1) * PAIR]
        k2 = qk[rows, W_MLSTM + p * PAIR:W_MLSTM + (p + 1) * PAIR] * (HEAD_DIM ** -0.5)
        v2 = proj[rows, ML_QK + p * PAIR:ML_QK + (p + 1) * PAIR]
        return q2, k2, v2

    st = {}
    for p in pairs:
        m_state = m_ref[p]
        for c in chunks:
            rows = slice(c * CHUNK, (c + 1) * CHUNK)
            b2 = _expand(bcum_i[rows], 2 * p, 2 * p + 1)
            i2 = _expand(i_pre[rows], 2 * p, 2 * p + 1)
            d_max = b2 + _expand(run_max[rows], 2 * p, 2 * p + 1)
            b_last = b2[CHUNK - 1:CHUNK, :]
            w_end = b_last - b2 + i2
            inter = b2 + m_state
            m_t = jnp.maximum(inter, d_max)
            inter_end = b_last + m_state
            m_new = jnp.maximum(inter_end, jnp.max(w_end, axis=0, keepdims=True))
            d2 = jnp.where(incl2, b2 + _row_form(i2 - b2, eye2), -jnp.inf)
            st[p, c] = dict(m_t=m_t, a_inter=jnp.exp(inter - m_t), decay=jnp.exp(d2 - m_t),
                            s=jnp.exp(w_end - m_new), dec=jnp.exp(inter_end - m_new))
            m_state = m_new
        m_ref[p] = m_state

    for p, c in units:
        q2, k2, v2 = operands(p, c)
        e = st[p, c]
        e["p2"] = _pmm_nt(q2, k2) * e["decay"]
        sk = e["s"] * k2
        e["sk_t"] = _bd(sk, F32).T
        e["ksum"] = jnp.sum(sk, axis=0, keepdims=True)
    for p, c in units:
        e = st[p, c]
        both = _dot(jnp.concatenate([e["p2"], e["sk_t"]], axis=0), _bd(operands(p, c)[2]))
        e["intra"], e["kv"] = both[:CHUNK], both[CHUNK:]

    for p in pairs:
        c_state, n_state = c_ref[p], n_ref[p]
        for c in chunks:
            e = st[p, c]
            e["c_in"], e["n_in"] = c_state, n_state
            c_state = e["dec"] * c_state + e["kv"]
            n_state = e["dec"] * n_state + e["ksum"]
        c_ref[p], n_ref[p] = c_state, n_state

    for p, c in units:
        q2 = operands(p, c)[0]
        e = st[p, c]
        e["num"] = e["a_inter"] * _dot(q2, e["c_in"]) + e["intra"]
        e["den_terms"] = e["a_inter"] * q2 * e["n_in"] + e["p2"]
    dens = _segsums([st[u]["den_terms"] for u in units])
    hids = []
    for (p, c), den in zip(units, dens):
        rows = slice(c * CHUNK, (c + 1) * CHUNK)
        e = st[p, c]
        hid = e["num"] / jnp.maximum(jnp.abs(den), jnp.exp(-e["m_t"]))
        hids.append(hid * _sigmoid(proj[rows, ML_QK + W_MLSTM + p * PAIR:ML_QK + W_MLSTM + (p + 1) * PAIR]))
    for u, hid, total in zip(units, hids, _segsums(hids)):
        st[u]["cen"] = hid - total * (1.0 / HEAD_DIM)
    for u, total in zip(units, _segsums([st[u]["cen"] * st[u]["cen"] for u in units])):
        st[u]["var"] = total * (1.0 / HEAD_DIM)
    for p, c in units:
        rows = slice(c * CHUNK, (c + 1) * CHUNK)
        lanes = slice(p * PAIR, (p + 1) * PAIR)
        e = st[p, c]
        o_ref[0, rows, lanes] = (e["cen"] * lax.rsqrt(e["var"] + NORM_EPS)
                                 * norm_ref[:, lanes]).astype(o_ref.dtype)


def _mlstm(x, mod, w, conv, par, norm):
    b, t, d = x.shape
    const = lambda i, j: (0, 0)
    return pl.pallas_call(
        _mlstm_kernel,
        grid=(b, t // MIX_ROWS),
        in_specs=[
            pl.BlockSpec((1, MIX_ROWS, d), lambda i, j: (i, j, 0)),
            pl.BlockSpec((1, 3, d), lambda i, j: (i, 0, 0)),
            pl.BlockSpec(w.shape, const),
            pl.BlockSpec(conv.shape, const),
            pl.BlockSpec(par.shape, const),
            pl.BlockSpec(norm.shape, const),
        ],
        out_specs=pl.BlockSpec((1, MIX_ROWS, W_MLSTM), lambda i, j: (i, j, 0)),
        out_shape=jax.ShapeDtypeStruct((b, t, W_MLSTM), BF16),
        scratch_shapes=[
            pltpu.VMEM((CARRY_ROWS, ML_QK), F32),
            pltpu.VMEM((H_MLSTM // 2, PAIR, PAIR), F32),
            pltpu.VMEM((H_MLSTM // 2, 1, PAIR), F32),
            pltpu.VMEM((H_MLSTM // 2, 1, PAIR), F32),
        ],
        compiler_params=pltpu.CompilerParams(vmem_limit_bytes=VMEM_LIMIT_BYTES),
        name="mlstm",
    )(x, mod, w, conv, par, norm)


def _moba_kernel(x_ref, mod_ref, w_ref, o_ref, k_ref, vt_ref, kmean_ref,
                 s0_ref, s1_ref, e0_ref, e1_ref, acc_ref):
    t = pl.program_id(1)
    n_blk = kmean_ref.shape[0]

    @pl.when(t == 0)
    def _():
        kmean_ref[...] = jnp.zeros_like(kmean_ref)

    proj = jnp.dot(_modulated(x_ref, mod_ref), w_ref[...], preferred_element_type=F32)
    q = proj[:, :W_MOBA]
    k = proj[:, W_MOBA:2 * W_MOBA]
    v = proj[:, 2 * W_MOBA:]
    k_ref[t] = k.astype(BF16)
    ones_rows = jnp.where(_row((VT_ROWS - PAIR, MOBA_BLOCK)) == 0, 1.0, 0.0)
    for p in range(H_MOBA // 2):
        vt_ref[t, p * VT_ROWS:(p + 1) * VT_ROWS, :] = jnp.concatenate(
            [v[:, p * PAIR:(p + 1) * PAIR].T, ones_rows], axis=0).astype(BF16)
    k_mean = jnp.mean(k, axis=0, keepdims=True)
    kmean_ref[...] = jnp.where(_row(kmean_ref.shape) == t, k_mean, kmean_ref[...])
    kmean = kmean_ref[...]

    n_q = 2 * MOBA_BLOCK
    blk = _row((n_blk, n_q)).astype(F32)
    t_f = t.astype(F32)
    causal = _row((MOBA_BLOCK, n_q)) <= (_lane((MOBA_BLOCK, n_q)) & (MOBA_BLOCK - 1))
    left = _lane((MOBA_BLOCK, PAIR)) < HEAD_DIM

    pairs = range(H_MOBA // 2)
    s_bufs, e_bufs = (s0_ref, s1_ref), (e0_ref, e1_ref)
    qs_all, picks_all, init = [], [], []
    for p in pairs:
        lanes = slice(p * PAIR, (p + 1) * PAIR)
        q_bd = _bd(q[:, lanes], F32)
        q_hi, q_lo = _split(q_bd)
        m_hi, m_lo = _split(kmean[:, lanes])
        gate = _dot_nt(m_hi, q_hi) + _dot_nt(m_hi, q_lo) + _dot_nt(m_lo, q_hi)
        gate = jnp.where(blk < t_f, gate, -jnp.inf)
        picks = []
        for r in range(MOBA_TOPK):
            best = jnp.max(gate, axis=0, keepdims=True)
            idx = jnp.min(jnp.where(gate == best, blk, float(n_blk)), axis=0, keepdims=True)
            gate = jnp.where(blk == idx, -jnp.inf, gate)
            picks.append(jnp.where(r < t, idx, -1.0))
        qs = (q_bd * (HEAD_DIM ** -0.5 * LOG2_E)).astype(BF16)
        s = jnp.where(causal, _dot_nt(k[:, lanes], qs), -jnp.inf)
        m_run = jnp.max(s, axis=0, keepdims=True)
        e = jnp.exp2(s - m_run)
        acc_ref[p] = _dot(vt_ref[t, p * VT_ROWS:(p + 1) * VT_ROWS, :], e)
        s_bufs[0][p] = _dot_nt(k_ref[0, :, lanes], qs)
        e_bufs[1][p] = jnp.zeros((MOBA_BLOCK, n_q), BF16)
        init.append((m_run, jnp.ones((1, n_q), F32)))
        qs_all.append(qs)
        picks_all.append(picks)

    def step(i, p, state, cur, nxt):
        m_run, scale_prev = state
        rows = slice(p * PAIR, (p + 1) * PAIR)
        vt_rows = slice(p * VT_ROWS, (p + 1) * VT_ROWS)
        s_bufs[nxt][p] = _dot_nt(k_ref[jnp.minimum(i + 1, t), :, rows], qs_all[p])
        acc_ref[p] = acc_ref[p] * scale_prev + _dot(vt_ref[jnp.maximum(i - 1, 0), vt_rows, :], e_bufs[nxt][p])
        i_f = i.astype(F32)
        chosen = (picks_all[p][0] == i_f) | (picks_all[p][1] == i_f) | (picks_all[p][2] == i_f)
        bias = jnp.where(chosen, 0.0, -jnp.inf)
        m_new = jnp.maximum(m_run, jnp.max(s_bufs[cur][p], axis=0, keepdims=True) + bias)
        e_bufs[cur][p] = jnp.exp2(s_bufs[cur][p] + (bias - m_new)).astype(BF16)
        return m_new, jnp.exp2(m_run - m_new)

    def body(i2, carry):
        carry = tuple(step(2 * i2, p, carry[p], 0, 1) for p in pairs)
        return tuple(step(2 * i2 + 1, p, carry[p], 1, 0) for p in pairs)

    trips = (t + 1) // 2
    final = lax.fori_loop(0, trips, body, tuple(init))
    last = jnp.minimum(jnp.maximum(2 * trips - 1, 0), t)
    for p in pairs:
        _, scale_prev = final[p]
        vt_rows = slice(p * VT_ROWS, (p + 1) * VT_ROWS)
        acc = acc_ref[p] * scale_prev + _dot(vt_ref[last, vt_rows, :], e_bufs[1][p])
        out = (acc[:PAIR] / acc[PAIR:PAIR + 1]).T
        o_ref[0, :, p * PAIR:(p + 1) * PAIR] = jnp.where(
            left, out[:MOBA_BLOCK], out[MOBA_BLOCK:]).astype(o_ref.dtype)


def _moba(x, mod, w):
    b, t, d = x.shape
    n_blk = t // MOBA_BLOCK
    assert n_blk >= MOBA_TOPK
    const = lambda i, j: (0, 0)
    return pl.pallas_call(
        _moba_kernel,
        grid=(b, n_blk),
        in_specs=[
            pl.BlockSpec((1, MOBA_BLOCK, d), lambda i, j: (i, j, 0)),
            pl.BlockSpec((1, 3, d), lambda i, j: (i, 0, 0)),
            pl.BlockSpec(w.shape, const),
        ],
        out_specs=pl.BlockSpec((1, MOBA_BLOCK, W_MOBA), lambda i, j: (i, j, 0)),
        out_shape=jax.ShapeDtypeStruct((b, t, W_MOBA), BF16),
        scratch_shapes=[
            pltpu.VMEM((n_blk, MOBA_BLOCK, W_MOBA), BF16),
            pltpu.VMEM((n_blk, (H_MOBA // 2) * VT_ROWS, MOBA_BLOCK), BF16),
            pltpu.VMEM((n_blk, W_MOBA), F32),
            pltpu.VMEM((H_MOBA // 2, MOBA_BLOCK, 2 * MOBA_BLOCK), F32),
            pltpu.VMEM((H_MOBA // 2, MOBA_BLOCK, 2 * MOBA_BLOCK), F32),
            pltpu.VMEM((H_MOBA // 2, MOBA_BLOCK, 2 * MOBA_BLOCK), BF16),
            pltpu.VMEM((H_MOBA // 2, MOBA_BLOCK, 2 * MOBA_BLOCK), BF16),
            pltpu.VMEM((H_MOBA // 2, VT_ROWS, 2 * MOBA_BLOCK), F32),
        ],
        compiler_params=pltpu.CompilerParams(vmem_limit_bytes=VMEM_LIMIT_BYTES),
        name="moba",
    )(x, mod, w)


def _mix_ffn_kernel(x_ref, mix_mod_ref, oa_ref, ob_ref, oc_ref, wo_ref, mix_g_ref, mix_b_ref,
                    mod_ref, w13_ref, w2_ref, g_ref, b_ref, o_ref):
    y = jnp.dot(oa_ref[0], wo_ref[:W_GDN, :], preferred_element_type=F32)
    y = y + jnp.dot(ob_ref[0], wo_ref[W_GDN:W_GDN + W_MOBA, :], preferred_element_type=F32)
    y = y + jnp.dot(oc_ref[0], wo_ref[W_GDN + W_MOBA:, :], preferred_element_type=F32)
    v = ALPHA * x_ref[0] + (1.0 + mix_mod_ref[0, 2:3, :]) * y
    x_mid = _layer_norm(v, mix_g_ref[...], mix_b_ref[...])
    o_ref[0] = _swiglu_sublayer(x_mid, mod_ref, w13_ref, w2_ref, g_ref, b_ref)


def _mix_ffn(x, mix_mod, oa, ob, oc, w_out, mix_g, mix_b, mod, w13, w2, ln_g, ln_b):
    b, t, d = x.shape
    const = lambda i, j: (0, 0)
    tile = lambda width: pl.BlockSpec((1, FFN_ROWS, width), lambda i, j: (i, j, 0))
    per_batch = pl.BlockSpec((1, 3, d), lambda i, j: (i, 0, 0))
    row = pl.BlockSpec((1, d), const)
    return pl.pallas_call(
        _mix_ffn_kernel,
        grid=(b, t // FFN_ROWS),
        in_specs=[
            tile(d), per_batch, tile(W_GDN), tile(W_MOBA), tile(W_MLSTM),
            pl.BlockSpec(w_out.shape, const, pipeline_mode=pl.Buffered(1)), row, row,
            per_batch,
            pl.BlockSpec(w13.shape, const, pipeline_mode=pl.Buffered(1)),
            pl.BlockSpec(w2.shape, const, pipeline_mode=pl.Buffered(1)), row, row,
        ],
        out_specs=tile(d),
        out_shape=jax.ShapeDtypeStruct(x.shape, F32),
        compiler_params=pltpu.CompilerParams(vmem_limit_bytes=VMEM_LIMIT_BYTES),
        name="mix_ffn",
    )(x, mix_mod, oa, ob, oc, w_out, mix_g.reshape(1, d), mix_b.reshape(1, d),
      mod, w13, w2, ln_g.reshape(1, d), ln_b.reshape(1, d))


def _small_cols(*cols):
    cat = jnp.concatenate(cols, axis=-1)
    return jnp.pad(cat, [(0, 0)] * (cat.ndim - 1) + [(0, SMALL - cat.shape[-1])])


def _split_w_in(w_in):
    sizes = [3 * W_GDN, H_GDN, H_GDN, W_GDN, 3 * W_MOBA,
             2 * W_MLSTM, W_MLSTM, H_MLSTM, H_MLSTM, W_MLSTM]
    parts, lo = [], 0
    for s in sizes:
        parts.append(w_in[:, lo:lo + s])
        lo += s
    qkv_a, a_a, b_a, z_a, qkv_b, qk_c, v_c, i_c, f_c, o_c = parts
    w_gdn = jnp.concatenate([qkv_a, z_a, _small_cols(a_a, b_a)], axis=1).astype(BF16)
    w_moba = qkv_b.astype(BF16)
    w_ml = jnp.concatenate([qk_c, v_c, o_c, _small_cols(i_c, f_c)], axis=1).astype(BF16)
    return w_gdn, w_moba, w_ml


def kernel(x, c, ada_w, ada_b, ffn_w13, ffn_w2, w_in, w_out, gdn_conv, gdn_a_log, gdn_dt_bias, gdn_norm,
           mlstm_conv, mlstm_i_bias, mlstm_f_bias, mlstm_norm, ln_g, ln_b):
    b = x.shape[0]
    depth = ada_w.shape[0]
    mod_all = _ada_mod(c, ada_w, ada_b).reshape(depth, b, N_SUB, 3, D_MODEL)
    for l in range(depth):
        mod = mod_all[l]
        w_gdn, w_moba, w_ml = _split_w_in(w_in[l])
        gdn_par = jnp.stack([_small_cols(gdn_a_log[l]), _small_cols(gdn_dt_bias[l])])
        gdn_par = jnp.pad(gdn_par, ((0, 6), (0, 0)))
        ml_par = jnp.pad(_small_cols(mlstm_i_bias[l], mlstm_f_bias[l])[None, :], ((0, 7), (0, 0)))
        gdn_norm2 = jnp.tile(gdn_norm[l], 2).reshape(1, PAIR)

        x = _ffn(x, mod[:, 0], ffn_w13[l, 0].astype(BF16), ffn_w2[l, 0].astype(BF16), ln_g[l, 0], ln_b[l, 0])
        oa = _gdn(x, mod[:, 1], w_gdn, gdn_conv[l], gdn_par, gdn_norm2)
        ob = _moba(x, mod[:, 1], w_moba)
        oc = _mlstm(x, mod[:, 1], w_ml, mlstm_conv[l], ml_par, mlstm_norm[l].reshape(1, W_MLSTM))
        x = _mix_ffn(x, mod[:, 1], oa, ob, oc, w_out[l].astype(BF16), ln_g[l, 1], ln_b[l, 1],
                     mod[:, 2], ffn_w13[l, 1].astype(BF16), ffn_w2[l, 1].astype(BF16), ln_g[l, 2], ln_b[l, 2])
    return x
```

```python
import jax
import jax.numpy as jnp
from jax import lax
from jax.experimental import pallas as pl
from jax.experimental.pallas import tpu as pltpu

F32 = jnp.float32
BF16 = jnp.bfloat16

D_MODEL = 1024
DEPTH = 2
HEAD_DIM = 64
PAIR = 2 * HEAD_DIM
H_GDN = 6
H_MOBA = 4
H_MLSTM = 6
W_GDN = H_GDN * HEAD_DIM
W_MOBA = H_MOBA * HEAD_DIM
W_MLSTM = H_MLSTM * HEAD_DIM
CONV_WIDTH = 4
CHUNK = 64
MOBA_BLOCK = 256
MOBA_TOPK = 3
D_FF = 2816
N_SUB = 3
ALPHA = (2 * DEPTH) ** 0.25
LN_EPS = 1e-5
NORM_EPS = 1e-6
LOG2_E = 1.4426950408889634
F32_SUBLANES = 8
BF16_SUBLANES = 16
SMALL = 128
CARRY_ROWS = F32_SUBLANES
VT_ROWS = PAIR + BF16_SUBLANES

VMEM_LIMIT_BYTES = 56 * 1024 * 1024
FFN_ROWS = 512
FFN_COLS = 256
MIX_ROWS = 512
ADA_COLS = 1152


def _sigmoid(x):
    return 1.0 / (1.0 + jnp.exp(-x))


def _silu(x):
    return x * _sigmoid(x)


def _softplus(x):
    return jnp.maximum(x, 0.0) + jnp.log1p(jnp.exp(-jnp.abs(x)))


def _layer_norm(v, g, b):
    mu = jnp.mean(v, -1, keepdims=True)
    d = v - mu
    var = jnp.mean(d * d, -1, keepdims=True)
    return d * lax.rsqrt(var + LN_EPS) * g + b


def _dot(a, b):
    return jnp.dot(a.astype(BF16), b.astype(BF16), preferred_element_type=F32)


def _dot_nt(a, b):
    return lax.dot_general(a.astype(BF16), b.astype(BF16), (((1,), (1,)), ((), ())),
                           preferred_element_type=F32)


def _split(x):
    hi = x.astype(BF16)
    lo = (x - hi.astype(F32)).astype(BF16)
    return hi, lo


def _lane(shape):
    return lax.broadcasted_iota(jnp.int32, shape, len(shape) - 1)


def _row(shape):
    return lax.broadcasted_iota(jnp.int32, shape, len(shape) - 2)


def _block_diag_mask():
    return (_row((PAIR, PAIR)) < HEAD_DIM) == (_lane((PAIR, PAIR)) < HEAD_DIM)


def _bd(r2, dtype=None):
    r2 = r2.astype(dtype or BF16)
    left = _lane(r2.shape) < HEAD_DIM
    return jnp.concatenate([jnp.where(left, r2, 0.0), jnp.where(left, 0.0, r2)], axis=0)


def _pmm(l2, r2):
    return _dot(l2, _bd(r2))


def _pmm_nt(l2, r2):
    return _dot_nt(l2, _bd(r2))


def _segsums(xs):
    ones = _block_diag_mask().astype(BF16)
    parts = [part for x in xs for part in _split(x)]
    sums = jnp.dot(jnp.concatenate(parts, axis=0), ones, preferred_element_type=F32)
    out, lo = [], 0
    for x in xs:
        n = x.shape[0]
        out.append(sums[lo:lo + n] + sums[lo + n:lo + 2 * n])
        lo += 2 * n
    return out


def _segsum(x):
    return _segsums([x])[0]


def _expand(cols, ja, jb):
    left = _lane((cols.shape[0], PAIR)) < HEAD_DIM
    return jnp.where(left, cols[:, ja:ja + 1], cols[:, jb:jb + 1])


def _chunk_scan(x, combine, identity):
    pos = _row(x.shape) & (CHUNK - 1)
    step = 1
    while step < CHUNK:
        x = combine(x, jnp.where(pos >= step, pltpu.roll(x, step, axis=0), identity))
        step *= 2
    return x


def _causal_conv(z, carry_ref, w_ref):
    rows = z.shape[0]
    prev = carry_ref[...]
    carry_ref[...] = z[rows - CARRY_ROWS:, :]
    head_rows = _row(prev.shape)
    y = z * w_ref[CONV_WIDTH - 1:CONV_WIDTH, :]
    for k in range(1, CONV_WIDTH):
        zr = pltpu.roll(z, k, axis=0)
        pr = pltpu.roll(prev, k, axis=0)
        head = jnp.where(head_rows < k, pr, zr[:CARRY_ROWS])
        shifted = jnp.concatenate([head, zr[CARRY_ROWS:]], axis=0)
        y = y + shifted * w_ref[CONV_WIDTH - 1 - k:CONV_WIDTH - k, :]
    return y


def _modulated(x_ref, mod_ref):
    return (x_ref[0] * (1.0 + mod_ref[0, 1:2, :]) + mod_ref[0, 0:1, :]).astype(BF16)


def _pair_masks():
    row = _row((CHUNK, PAIR))
    col = _lane((CHUNK, PAIR)) & (HEAD_DIM - 1)
    return col <= row, col < row, col == row


def _row_form(col2, eye2):
    return jnp.sum(jnp.where(eye2, col2, 0.0), axis=0, keepdims=True)


def _ada_kernel(c_ref, w_ref, b_ref, o_ref):
    o_ref[0] = _dot(_silu(c_ref[...]), w_ref[0]) + b_ref[0]


def _ada_mod(c, ada_w, ada_b):
    depth, d, n = ada_w.shape
    rows = c.shape[0]
    b = -(-rows // F32_SUBLANES) * F32_SUBLANES
    c = jnp.pad(c, ((0, b - rows), (0, 0)))
    return pl.pallas_call(
        _ada_kernel,
        grid=(depth, n // ADA_COLS),
        in_specs=[
            pl.BlockSpec((b, d), lambda l, j: (0, 0)),
            pl.BlockSpec((1, d, ADA_COLS), lambda l, j: (l, 0, j)),
            pl.BlockSpec((1, 1, ADA_COLS), lambda l, j: (l, 0, j)),
        ],
        out_specs=pl.BlockSpec((1, b, ADA_COLS), lambda l, j: (l, 0, j)),
        out_shape=jax.ShapeDtypeStruct((depth, b, n), F32),
        compiler_params=pltpu.CompilerParams(vmem_limit_bytes=VMEM_LIMIT_BYTES),
        name="ada_mod",
    )(c, ada_w, ada_b.reshape(depth, 1, n))[:, :rows]


def _swiglu_sublayer(x, mod_ref, w13_ref, w2_ref, g_ref, b_ref):
    h = (x * (1.0 + mod_ref[0, 1:2, :]) + mod_ref[0, 0:1, :]).astype(BF16)
    acc = jnp.zeros(x.shape, F32)
    for c in range(D_FF // FFN_COLS):
        lo = c * FFN_COLS
        a = jnp.dot(h, w13_ref[:, lo:lo + FFN_COLS], preferred_element_type=F32)
        b = jnp.dot(h, w13_ref[:, D_FF + lo:D_FF + lo + FFN_COLS], preferred_element_type=F32)
        act = (_silu(a) * b).astype(BF16)
        acc = acc + jnp.dot(act, w2_ref[lo:lo + FFN_COLS, :], preferred_element_type=F32)
    y = ALPHA * x + (0.5 * (1.0 + mod_ref[0, 2:3, :])) * acc
    return _layer_norm(y, g_ref[...], b_ref[...])


def _ffn_kernel(x_ref, mod_ref, w13_ref, w2_ref, g_ref, b_ref, o_ref):
    o_ref[0] = _swiglu_sublayer(x_ref[0], mod_ref, w13_ref, w2_ref, g_ref, b_ref)


def _ffn(x, mod, w13, w2, ln_g, ln_b):
    b, t, d = x.shape
    const = lambda i, j: (0, 0)
    return pl.pallas_call(
        _ffn_kernel,
        grid=(b, t // FFN_ROWS),
        in_specs=[
            pl.BlockSpec((1, FFN_ROWS, d), lambda i, j: (i, j, 0)),
            pl.BlockSpec((1, 3, d), lambda i, j: (i, 0, 0)),
            pl.BlockSpec(w13.shape, const, pipeline_mode=pl.Buffered(1)),
            pl.BlockSpec(w2.shape, const, pipeline_mode=pl.Buffered(1)),
            pl.BlockSpec((1, d), const),
            pl.BlockSpec((1, d), const),
        ],
        out_specs=pl.BlockSpec((1, FFN_ROWS, d), lambda i, j: (i, j, 0)),
        out_shape=jax.ShapeDtypeStruct(x.shape, F32),
        compiler_params=pltpu.CompilerParams(vmem_limit_bytes=VMEM_LIMIT_BYTES),
        name="ffn",
    )(x, mod, w13, w2, ln_g.reshape(1, d), ln_b.reshape(1, d))


GDN_QKV = 3 * W_GDN


def _gdn_kernel(x_ref, mod_ref, w_ref, conv_ref, par_ref, norm_ref, o_ref, carry_ref, s_ref):
    @pl.when(pl.program_id(1) == 0)
    def _():
        carry_ref[...] = jnp.zeros_like(carry_ref)
        s_ref[...] = jnp.zeros_like(s_ref)

    proj = jnp.dot(_modulated(x_ref, mod_ref), w_ref[...], preferred_element_type=F32)
    qkv = _silu(_causal_conv(proj[:, :GDN_QKV], carry_ref, conv_ref))
    zs = proj[:, GDN_QKV + W_GDN:]
    g = -jnp.exp(par_ref[0:1, :]) * _softplus(zs + par_ref[1:2, :])
    beta = _sigmoid(zs)
    gcum = _chunk_scan(g, jnp.add, 0.0)
    incl2, strict2, eye2 = _pair_masks()
    eye_f = jnp.where(eye2, 1.0, 0.0)
    pairs = range(H_GDN // 2)
    chunks = range(MIX_ROWS // CHUNK)

    q_n, k_n = [], []
    for p in pairs:
        q_all = qkv[:, p * PAIR:(p + 1) * PAIR]
        k_all = qkv[:, W_GDN + p * PAIR:W_GDN + (p + 1) * PAIR]
        q_n.append(q_all * lax.rsqrt(_segsum(q_all * q_all) + NORM_EPS) * (HEAD_DIM ** -0.5))
        k_n.append(k_all * lax.rsqrt(_segsum(k_all * k_all) + NORM_EPS))
    st = {}

    def prepare(p, c):
        rows = slice(c * CHUNK, (c + 1) * CHUNK)
        q2, k2 = q_n[p][rows], k_n[p][rows]
        v2 = qkv[rows, 2 * W_GDN + p * PAIR:2 * W_GDN + (p + 1) * PAIR]
        g2 = _expand(gcum[rows], 2 * p, 2 * p + 1)
        b2 = _expand(beta[rows], H_GDN + 2 * p, H_GDN + 2 * p + 1)
        g_last = g2[CHUNK - 1:CHUNK, :]
        decay = jnp.exp(jnp.where(incl2, g2 - _row_form(g2, eye2), -jnp.inf))
        exp_g = jnp.exp(g2)
        kb = k2 * b2
        gram = _dot_nt(jnp.concatenate([kb, q2], axis=0), _bd(k2))
        st[p, c] = dict(
            xp=-jnp.where(strict2, gram[:CHUNK] * decay, 0.0),
            attn=jnp.where(incl2, gram[CHUNK:] * decay, 0.0),
            rhs=jnp.concatenate([_bd(v2 * b2), _bd(kb * exp_g)], axis=1),
            q_dec=q2 * exp_g,
            k_dec_t=_bd(k2 * jnp.exp(g_last - g2), F32).T,
            g_end=jnp.exp(g_last))

    def inverse_rounds(group):
        tinv = {u: eye_f + st[u]["xp"] for u in group}

        def square():
            for u in group:
                st[u]["xp"] = _pmm(st[u]["xp"], st[u]["xp"])

        def fold_and_square():
            for u in group:
                x = st[u]["xp"]
                both = _pmm(jnp.concatenate([tinv[u], x], axis=0), x)
                tinv[u] = tinv[u] + both[:CHUNK]
                st[u]["xp"] = both[CHUNK:]

        def fold():
            for u in group:
                tinv[u] = tinv[u] + _pmm(tinv[u], st[u]["xp"])

        return tinv, [square] + [fold_and_square] * 4 + [fold]

    def solve(group, tinv):
        for u in group:
            uw = _dot(tinv[u], st[u]["rhs"])
            st[u]["u"], st[u]["w"] = uw[:, :PAIR], uw[:, PAIR:]

    def mean_square(c):
        sums = _segsums([st[p, c]["o"] * st[p, c]["o"] for p in pairs])
        for p in pairs:
            st[p, c]["ms"] = sums[p] * (1.0 / HEAD_DIM)

    def emit(c):
        rows = slice(c * CHUNK, (c + 1) * CHUNK)
        for p in pairs:
            e = st[p, c]
            z2 = proj[rows, GDN_QKV + p * PAIR:GDN_QKV + (p + 1) * PAIR]
            o_ref[0, rows, p * PAIR:(p + 1) * PAIR] = (
                e["o"] * lax.rsqrt(e["ms"] + NORM_EPS) * norm_ref[...] * _silu(z2)).astype(o_ref.dtype)

    states = [s_ref[p] for p in pairs]

    def recur(c):
        ws_qs = [_dot(jnp.concatenate([st[p, c]["w"], st[p, c]["q_dec"]], axis=0), states[p])
                 for p in pairs]
        if c > 0:
            mean_square(c - 1)
        for p in pairs:
            e = st[p, c]
            v_new = e["u"] - ws_qs[p][:CHUNK]
            both = _dot(jnp.concatenate([e["attn"], e["k_dec_t"]], axis=0), _bd(v_new))
            e["o"] = ws_qs[p][CHUNK:] + both[:CHUNK]
            states[p] = states[p] * e["g_end"] + both[CHUNK:]
        if c > 0:
            emit(c - 1)

    half = len(chunks) // 2
    first = [(p, c) for c in chunks[:half] for p in pairs]
    second = [(p, c) for c in chunks[half:] for p in pairs]
    for u in first:
        prepare(*u)
    tinv, rounds = inverse_rounds(first)
    share = -(-len(second) // len(rounds))
    for r, run in enumerate(rounds):
        run()
        for u in second[r * share:(r + 1) * share]:
            prepare(*u)
    solve(first, tinv)
    tinv, rounds = inverse_rounds(second)
    for r, run in enumerate(rounds):
        run()
        if r < half:
            recur(chunks[r])
    solve(second, tinv)
    for c in chunks[min(len(rounds), half):]:
        recur(c)
    for p in pairs:
        s_ref[p] = states[p]
    mean_square(chunks[-1])
    emit(chunks[-1])


def _gdn(x, mod, w, conv, par, norm):
    b, t, d = x.shape
    const = lambda i, j: (0, 0)
    return pl.pallas_call(
        _gdn_kernel,
        grid=(b, t // MIX_ROWS),
        in_specs=[
            pl.BlockSpec((1, MIX_ROWS, d), lambda i, j: (i, j, 0)),
            pl.BlockSpec((1, 3, d), lambda i, j: (i, 0, 0)),
            pl.BlockSpec(w.shape, const),
            pl.BlockSpec(conv.shape, const),
            pl.BlockSpec(par.shape, const),
            pl.BlockSpec(norm.shape, const),
        ],
        out_specs=pl.BlockSpec((1, MIX_ROWS, W_GDN), lambda i, j: (i, j, 0)),
        out_shape=jax.ShapeDtypeStruct((b, t, W_GDN), BF16),
        scratch_shapes=[
            pltpu.VMEM((CARRY_ROWS, GDN_QKV), F32),
            pltpu.VMEM((H_GDN // 2, PAIR, PAIR), F32),
        ],
        compiler_params=pltpu.CompilerParams(vmem_limit_bytes=VMEM_LIMIT_BYTES),
        name="gdn",
    )(x, mod, w, conv, par, norm)


ML_QK = 2 * W_MLSTM


def _mlstm_kernel(x_ref, mod_ref, w_ref, conv_ref, par_ref, norm_ref, o_ref,
                  carry_ref, c_ref, n_ref, m_ref):
    @pl.when(pl.program_id(1) == 0)
    def _():
        carry_ref[...] = jnp.zeros_like(carry_ref)
        c_ref[...] = jnp.zeros_like(c_ref)
        n_ref[...] = jnp.zeros_like(n_ref)
        m_ref[...] = jnp.zeros_like(m_ref)

    proj = jnp.dot(_modulated(x_ref, mod_ref), w_ref[...], preferred_element_type=F32)
    qk = _silu(_causal_conv(proj[:, :ML_QK], carry_ref, conv_ref))
    zs = proj[:, ML_QK + 2 * W_MLSTM:] + par_ref[0:1, :]
    i_pre = zs
    logf = -_softplus(-zs)
    bcum = _chunk_scan(logf, jnp.add, 0.0)
    bcum_i = pltpu.roll(bcum, SMALL - H_MLSTM, axis=1)
    i_minus_b = i_pre - bcum_i
    run_max = _chunk_scan(i_minus_b, jnp.maximum, -jnp.inf)
    incl2, _, eye2 = _pair_masks()
    pairs = range(H_MLSTM // 2)
    chunks = range(MIX_ROWS // CHUNK)
    units = [(p, c) for p in pairs for c in chunks]

    def operands(p, c):
        rows = slice(c * CHUNK, (c + 1) * CHUNK)
        q2 = qk[rows, p * PAIR:(p + 1) * PAIR]
        k2 = qk[rows, W_MLSTM + p * PAIR:W_MLSTM + (p + 1) * PAIR] * (HEAD_DIM ** -0.5)
        v2 = proj[rows, ML_QK + p * PAIR:ML_QK + (p + 1) * PAIR]
        return q2, k2, v2

    st = {}
    for p in pairs:
        m_state = m_ref[p]
        for c in chunks:
            rows = slice(c * CHUNK, (c + 1) * CHUNK)
            b2 = _expand(bcum_i[rows], 2 * p, 2 * p + 1)
            i2 = _expand(i_pre[rows], 2 * p, 2 * p + 1)
            d_max = b2 + _expand(run_max[rows], 2 * p, 2 * p + 1)
            b_last = b2[CHUNK - 1:CHUNK, :]
            w_end = b_last - b2 + i2
            inter = b2 + m_state
            m_t = jnp.maximum(inter, d_max)
            inter_end = b_last + m_state
            m_new = jnp.maximum(inter_end, jnp.max(w_end, axis=0, keepdims=True))
            d2 = jnp.where(incl2, b2 + _row_form(i2 - b2, eye2), -jnp.inf)
            st[p, c] = dict(m_t=m_t, a_inter=jnp.exp(inter - m_t), decay=jnp.exp(d2 - m_t),
                            s=jnp.exp(w_end - m_new), dec=jnp.exp(inter_end - m_new))
            m_state = m_new
        m_ref[p] = m_state

    for p, c in units:
        q2, k2, v2 = operands(p, c)
        e = st[p, c]
        e["p2"] = _pmm_nt(q2, k2) * e["decay"]
        sk = e["s"] * k2
        e["sk_t"] = _bd(sk, F32).T
        e["ksum"] = jnp.sum(sk, axis=0, keepdims=True)
    for p, c in units:
        e = st[p, c]
        both = _dot(jnp.concatenate([e["p2"], e["sk_t"]], axis=0), _bd(operands(p, c)[2]))
        e["intra"], e["kv"] = both[:CHUNK], both[CHUNK:]

    for p in pairs:
        c_state, n_state = c_ref[p], n_ref[p]
        for c in chunks:
            e = st[p, c]
            e["c_in"], e["n_in"] = c_state, n_state
            c_state = e["dec"] * c_state + e["kv"]
            n_state = e["dec"] * n_state + e["ksum"]
        c_ref[p], n_ref[p] = c_state, n_state

    for p, c in units:
        q2 = operands(p, c)[0]
        e = st[p, c]
        e["num"] = e["a_inter"] * _dot(q2, e["c_in"]) + e["intra"]
        e["den_terms"] = e["a_inter"] * q2 * e["n_in"] + e["p2"]
    dens = _segsums([st[u]["den_terms"] for u in units])
    hids = []
    for (p, c), den in zip(units, dens):
        rows = slice(c * CHUNK, (c + 1) * CHUNK)
        e = st[p, c]
        hid = e["num"] / jnp.maximum(jnp.abs(den), jnp.exp(-e["m_t"]))
        hids.append(hid * _sigmoid(proj[rows, ML_QK + W_MLSTM + p * PAIR:ML_QK + W_MLSTM + (p + 1) * PAIR]))
    for u, hid, total in zip(units, hids, _segsums(hids)):
        st[u]["cen"] = hid - total * (1.0 / HEAD_DIM)
    for u, total in zip(units, _segsums([st[u]["cen"] * st[u]["cen"] for u in units])):
        st[u]["var"] = total * (1.0 / HEAD_DIM)
    for p, c in units:
        rows = slice(c * CHUNK, (c + 1) * CHUNK)
        lanes = slice(p * PAIR, (p + 1) * PAIR)
        e = st[p, c]
        o_ref[0, rows, lanes] = (e["cen"] * lax.rsqrt(e["var"] + NORM_EPS)
                                 * norm_ref[:, lanes]).astype(o_ref.dtype)


def _mlstm(x, mod, w, conv, par, norm):
    b, t, d = x.shape
    const = lambda i, j: (0, 0)
    return pl.pallas_call(
        _mlstm_kernel,
        grid=(b, t // MIX_ROWS),
        in_specs=[
            pl.BlockSpec((1, MIX_ROWS, d), lambda i, j: (i, j, 0)),
            pl.BlockSpec((1, 3, d), lambda i, j: (i, 0, 0)),
            pl.BlockSpec(w.shape, const),
            pl.BlockSpec(conv.shape, const),
            pl.BlockSpec(par.shape, const),
            pl.BlockSpec(norm.shape, const),
        ],
        out_specs=pl.BlockSpec((1, MIX_ROWS, W_MLSTM), lambda i, j: (i, j, 0)),
        out_shape=jax.ShapeDtypeStruct((b, t, W_MLSTM), BF16),
        scratch_shapes=[
            pltpu.VMEM((CARRY_ROWS, ML_QK), F32),
            pltpu.VMEM((H_MLSTM // 2, PAIR, PAIR), F32),
            pltpu.VMEM((H_MLSTM // 2, 1, PAIR), F32),
            pltpu.VMEM((H_MLSTM // 2, 1, PAIR), F32),
        ],
        compiler_params=pltpu.CompilerParams(vmem_limit_bytes=VMEM_LIMIT_BYTES),
        name="mlstm",
    )(x, mod, w, conv, par, norm)


def _moba_kernel(x_ref, mod_ref, w_ref, o_ref, k_ref, vt_ref, kmean_ref,
                 s0_ref, s1_ref, e0_ref, e1_ref, acc_ref):
    t = pl.program_id(1)
    n_blk = kmean_ref.shape[0]

    @pl.when(t == 0)
    def _():
        kmean_ref[...] = jnp.zeros_like(kmean_ref)

    proj = jnp.dot(_modulated(x_ref, mod_ref), w_ref[...], preferred_element_type=F32)
    q = proj[:, :W_MOBA]
    k = proj[:, W_MOBA:2 * W_MOBA]
    v = proj[:, 2 * W_MOBA:]
    k_ref[t] = k.astype(BF16)
    ones_rows = jnp.where(_row((VT_ROWS - PAIR, MOBA_BLOCK)) == 0, 1.0, 0.0)
    for p in range(H_MOBA // 2):
        vt_ref[t, p * VT_ROWS:(p + 1) * VT_ROWS, :] = jnp.concatenate(
            [v[:, p * PAIR:(p + 1) * PAIR].T, ones_rows], axis=0).astype(BF16)
    k_mean = jnp.mean(k, axis=0, keepdims=True)
    kmean_ref[...] = jnp.where(_row(kmean_ref.shape) == t, k_mean, kmean_ref[...])
    kmean = kmean_ref[...]

    n_q = 2 * MOBA_BLOCK
    blk = _row((n_blk, n_q)).astype(F32)
    t_f = t.astype(F32)
    causal = _row((MOBA_BLOCK, n_q)) <= (_lane((MOBA_BLOCK, n_q)) & (MOBA_BLOCK - 1))
    left = _lane((MOBA_BLOCK, PAIR)) < HEAD_DIM

    pairs = range(H_MOBA // 2)
    s_bufs, e_bufs = (s0_ref, s1_ref), (e0_ref, e1_ref)
    lanes = [slice(p * PAIR, (p + 1) * PAIR) for p in pairs]
    q_bd = [_bd(q[:, lanes[p]], F32) for p in pairs]
    gates = []
    for p in pairs:
        q_hi, q_lo = _split(q_bd[p])
        m_hi, m_lo = _split(kmean[:, lanes[p]])
        gate = _dot_nt(m_hi, q_hi) + _dot_nt(m_hi, q_lo) + _dot_nt(m_lo, q_hi)
        gates.append(jnp.where(blk < t_f, gate, -jnp.inf))
    qs_all = [(q_bd[p] * (HEAD_DIM ** -0.5 * LOG2_E)).astype(BF16) for p in pairs]
    own = [jnp.where(causal, _dot_nt(k[:, lanes[p]], qs_all[p]), -jnp.inf) for p in pairs]
    picks_all = []
    for p in pairs:
        gate, picks = gates[p], []
        for r in range(MOBA_TOPK):
            best = jnp.max(gate, axis=0, keepdims=True)
            idx = jnp.min(jnp.where(gate == best, blk, float(n_blk)), axis=0, keepdims=True)
            gate = jnp.where(blk == idx, -jnp.inf, gate)
            picks.append(jnp.where(r < t, idx, -1.0))
        picks_all.append(picks)
    init = []
    for p in pairs:
        m_run = jnp.max(own[p], axis=0, keepdims=True)
        e = jnp.exp2(own[p] - m_run)
        acc_ref[p] = _dot(vt_ref[t, p * VT_ROWS:(p + 1) * VT_ROWS, :], e)
        s_bufs[0][p] = _dot_nt(k_ref[0, :, lanes[p]], qs_all[p])
        e_bufs[1][p] = jnp.zeros((MOBA_BLOCK, n_q), BF16)
        init.append((m_run, jnp.ones((1, n_q), F32)))

    def step(i, states, cur, nxt):
        i_f = i.astype(F32)
        for p in pairs:
            vt_rows = slice(p * VT_ROWS, (p + 1) * VT_ROWS)
            acc_ref[p] = acc_ref[p] * states[p][1] + _dot(
                vt_ref[jnp.maximum(i - 1, 0), vt_rows, :], e_bufs[nxt][p])
        new = []
        for p in pairs:
            m_run = states[p][0]
            chosen = (picks_all[p][0] == i_f) | (picks_all[p][1] == i_f) | (picks_all[p][2] == i_f)
            bias = jnp.where(chosen, 0.0, -jnp.inf)
            m_new = jnp.maximum(m_run, jnp.max(s_bufs[cur][p], axis=0, keepdims=True) + bias)
            e_bufs[cur][p] = jnp.exp2(s_bufs[cur][p] + (bias - m_new)).astype(BF16)
            new.append((m_new, jnp.exp2(m_run - m_new)))
        for p in pairs:
            rows = slice(p * PAIR, (p + 1) * PAIR)
            s_bufs[nxt][p] = _dot_nt(k_ref[jnp.minimum(i + 1, t), :, rows], qs_all[p])
        return tuple(new)

    def body(i2, carry):
        return step(2 * i2 + 1, step(2 * i2, carry, 0, 1), 1, 0)

    trips = (t + 1) // 2
    final = lax.fori_loop(0, trips, body, tuple(init))
    last = jnp.minimum(jnp.maximum(2 * trips - 1, 0), t)
    for p in pairs:
        _, scale_prev = final[p]
        vt_rows = slice(p * VT_ROWS, (p + 1) * VT_ROWS)
        acc = acc_ref[p] * scale_prev + _dot(vt_ref[last, vt_rows, :], e_bufs[1][p])
        out = (acc[:PAIR] / acc[PAIR:PAIR + 1]).T
        o_ref[0, :, p * PAIR:(p + 1) * PAIR] = jnp.where(
            left, out[:MOBA_BLOCK], out[MOBA_BLOCK:]).astype(o_ref.dtype)


def _moba(x, mod, w):
    b, t, d = x.shape
    n_blk = t // MOBA_BLOCK
    assert n_blk >= MOBA_TOPK
    const = lambda i, j: (0, 0)
    return pl.pallas_call(
        _moba_kernel,
        grid=(b, n_blk),
        in_specs=[
            pl.BlockSpec((1, MOBA_BLOCK, d), lambda i, j: (i, j, 0)),
            pl.BlockSpec((1, 3, d), lambda i, j: (i, 0, 0)),
            pl.BlockSpec(w.shape, const),
        ],
        out_specs=pl.BlockSpec((1, MOBA_BLOCK, W_MOBA), lambda i, j: (i, j, 0)),
        out_shape=jax.ShapeDtypeStruct((b, t, W_MOBA), BF16),
        scratch_shapes=[
            pltpu.VMEM((n_blk, MOBA_BLOCK, W_MOBA), BF16),
            pltpu.VMEM((n_blk, (H_MOBA // 2) * VT_ROWS, MOBA_BLOCK), BF16),
            pltpu.VMEM((n_blk, W_MOBA), F32),
            pltpu.VMEM((H_MOBA // 2, MOBA_BLOCK, 2 * MOBA_BLOCK), F32),
            pltpu.VMEM((H_MOBA // 2, MOBA_BLOCK, 2 * MOBA_BLOCK), F32),
            pltpu.VMEM((H_MOBA // 2, MOBA_BLOCK, 2 * MOBA_BLOCK), BF16),
            pltpu.VMEM((H_MOBA // 2, MOBA_BLOCK, 2 * MOBA_BLOCK), BF16),
            pltpu.VMEM((H_MOBA // 2, VT_ROWS, 2 * MOBA_BLOCK), F32),
        ],
        compiler_params=pltpu.CompilerParams(vmem_limit_bytes=VMEM_LIMIT_BYTES),
        name="moba",
    )(x, mod, w)


def _mix_ffn_kernel(x_ref, mix_mod_ref, oa_ref, ob_ref, oc_ref, wo_ref, mix_g_ref, mix_b_ref,
                    mod_ref, w13_ref, w2_ref, g_ref, b_ref, o_ref):
    y = jnp.dot(oa_ref[0], wo_ref[:W_GDN, :], preferred_element_type=F32)
    y = y + jnp.dot(ob_ref[0], wo_ref[W_GDN:W_GDN + W_MOBA, :], preferred_element_type=F32)
    y = y + jnp.dot(oc_ref[0], wo_ref[W_GDN + W_MOBA:, :], preferred_element_type=F32)
    v = ALPHA * x_ref[0] + (1.0 + mix_mod_ref[0, 2:3, :]) * y
    x_mid = _layer_norm(v, mix_g_ref[...], mix_b_ref[...])
    o_ref[0] = _swiglu_sublayer(x_mid, mod_ref, w13_ref, w2_ref, g_ref, b_ref)


def _mix_ffn(x, mix_mod, oa, ob, oc, w_out, mix_g, mix_b, mod, w13, w2, ln_g, ln_b):
    b, t, d = x.shape
    const = lambda i, j: (0, 0)
    tile = lambda width: pl.BlockSpec((1, FFN_ROWS, width), lambda i, j: (i, j, 0))
    per_batch = pl.BlockSpec((1, 3, d), lambda i, j: (i, 0, 0))
    row = pl.BlockSpec((1, d), const)
    return pl.pallas_call(
        _mix_ffn_kernel,
        grid=(b, t // FFN_ROWS),
        in_specs=[
            tile(d), per_batch, tile(W_GDN), tile(W_MOBA), tile(W_MLSTM),
            pl.BlockSpec(w_out.shape, const, pipeline_mode=pl.Buffered(1)), row, row,
            per_batch,
            pl.BlockSpec(w13.shape, const, pipeline_mode=pl.Buffered(1)),
            pl.BlockSpec(w2.shape, const, pipeline_mode=pl.Buffered(1)), row, row,
        ],
        out_specs=tile(d),
        out_shape=jax.ShapeDtypeStruct(x.shape, F32),
        compiler_params=pltpu.CompilerParams(vmem_limit_bytes=VMEM_LIMIT_BYTES),
        name="mix_ffn",
    )(x, mix_mod, oa, ob, oc, w_out, mix_g.reshape(1, d), mix_b.reshape(1, d),
      mod, w13, w2, ln_g.reshape(1, d), ln_b.reshape(1, d))


def _small_cols(*cols):
    cat = jnp.concatenate(cols, axis=-1)
    return jnp.pad(cat, [(0, 0)] * (cat.ndim - 1) + [(0, SMALL - cat.shape[-1])])


def _split_w_in(w_in):
    sizes = [3 * W_GDN, H_GDN, H_GDN, W_GDN, 3 * W_MOBA,
             2 * W_MLSTM, W_MLSTM, H_MLSTM, H_MLSTM, W_MLSTM]
    parts, lo = [], 0
    for s in sizes:
        parts.append(w_in[:, lo:lo + s])
        lo += s
    qkv_a, a_a, b_a, z_a, qkv_b, qk_c, v_c, i_c, f_c, o_c = parts
    w_gdn = jnp.concatenate([qkv_a, z_a, _small_cols(a_a, b_a)], axis=1).astype(BF16)
    w_moba = qkv_b.astype(BF16)
    w_ml = jnp.concatenate([qk_c, v_c, o_c, _small_cols(i_c, f_c)], axis=1).astype(BF16)
    return w_gdn, w_moba, w_ml


def kernel(x, c, ada_w, ada_b, ffn_w13, ffn_w2, w_in, w_out, gdn_conv, gdn_a_log, gdn_dt_bias, gdn_norm,
           mlstm_conv, mlstm_i_bias, mlstm_f_bias, mlstm_norm, ln_g, ln_b):
    b = x.shape[0]
    depth = ada_w.shape[0]
    mod_all = _ada_mod(c, ada_w, ada_b).reshape(depth, b, N_SUB, 3, D_MODEL)
    for l in range(depth):
        mod = mod_all[l]
        w_gdn, w_moba, w_ml = _split_w_in(w_in[l])
        gdn_par = jnp.stack([_small_cols(gdn_a_log[l]), _small_cols(gdn_dt_bias[l])])
        gdn_par = jnp.pad(gdn_par, ((0, 6), (0, 0)))
        ml_par = jnp.pad(_small_cols(mlstm_i_bias[l], mlstm_f_bias[l])[None, :], ((0, 7), (0, 0)))
        gdn_norm2 = jnp.tile(gdn_norm[l], 2).reshape(1, PAIR)

        x = _ffn(x, mod[:, 0], ffn_w13[l, 0].astype(BF16), ffn_w2[l, 0].astype(BF16), ln_g[l, 0], ln_b[l, 0])
        oa = _gdn(x, mod[:, 1], w_gdn, gdn_conv[l], gdn_par, gdn_norm2)
        ob = _moba(x, mod[:, 1], w_moba)
        oc = _mlstm(x, mod[:, 1], w_ml, mlstm_conv[l], ml_par, mlstm_norm[l].reshape(1, W_MLSTM))
        x = _mix_ffn(x, mod[:, 1], oa, ob, oc, w_out[l].astype(BF16), ln_g[l, 1], ln_b[l, 1],
                     mod[:, 2], ffn_w13[l, 1].astype(BF16), ffn_w2[l, 1].astype(BF16), ln_g[l, 2], ln_b[l, 2])
    return x
```

```python
import jax
import jax.numpy as jnp
from jax import lax
from jax.experimental import pallas as pl
from jax.experimental.pallas import tpu as pltpu

F32 = jnp.float32
BF16 = jnp.bfloat16

D_MODEL = 1024
DEPTH = 2
HEAD_DIM = 64
PAIR = 2 * HEAD_DIM
H_GDN = 6
H_MOBA = 4
H_MLSTM = 6
W_GDN = H_GDN * HEAD_DIM
W_MOBA = H_MOBA * HEAD_DIM
W_MLSTM = H_MLSTM * HEAD_DIM
CONV_WIDTH = 4
CHUNK = 64
MOBA_BLOCK = 256
MOBA_TOPK = 3
D_FF = 2816
N_SUB = 3
ALPHA = (2 * DEPTH) ** 0.25
LN_EPS = 1e-5
NORM_EPS = 1e-6
LOG2_E = 1.4426950408889634
F32_SUBLANES = 8
BF16_SUBLANES = 16
SMALL = 128
CARRY_ROWS = F32_SUBLANES
VT_ROWS = PAIR + BF16_SUBLANES

VMEM_LIMIT_BYTES = 56 * 1024 * 1024
FFN_ROWS = 512
FFN_COLS = 256
MIX_ROWS = 512
ADA_COLS = 1152


def _sigmoid(x):
    return 1.0 / (1.0 + jnp.exp(-x))


def _silu(x):
    return x * _sigmoid(x)


def _softplus(x):
    return jnp.maximum(x, 0.0) + jnp.log1p(jnp.exp(-jnp.abs(x)))


def _layer_norm(v, g, b):
    mu = jnp.mean(v, -1, keepdims=True)
    d = v - mu
    var = jnp.mean(d * d, -1, keepdims=True)
    return d * lax.rsqrt(var + LN_EPS) * g + b


def _dot(a, b):
    return jnp.dot(a.astype(BF16), b.astype(BF16), preferred_element_type=F32)


def _dot_nt(a, b):
    return lax.dot_general(a.astype(BF16), b.astype(BF16), (((1,), (1,)), ((), ())),
                           preferred_element_type=F32)


def _split(x):
    hi = x.astype(BF16)
    lo = (x - hi.astype(F32)).astype(BF16)
    return hi, lo


def _lane(shape):
    return lax.broadcasted_iota(jnp.int32, shape, len(shape) - 1)


def _row(shape):
    return lax.broadcasted_iota(jnp.int32, shape, len(shape) - 2)


def _block_diag_mask():
    return (_row((PAIR, PAIR)) < HEAD_DIM) == (_lane((PAIR, PAIR)) < HEAD_DIM)


def _bd(r2, dtype=None):
    r2 = r2.astype(dtype or BF16)
    left = _lane(r2.shape) < HEAD_DIM
    return jnp.concatenate([jnp.where(left, r2, 0.0), jnp.where(left, 0.0, r2)], axis=0)


def _pmm(l2, r2):
    return _dot(l2, _bd(r2))


def _pmm_nt(l2, r2):
    return _dot_nt(l2, _bd(r2))


def _segsums(xs):
    ones = _block_diag_mask().astype(BF16)
    parts = [part for x in xs for part in _split(x)]
    sums = jnp.dot(jnp.concatenate(parts, axis=0), ones, preferred_element_type=F32)
    out, lo = [], 0
    for x in xs:
        n = x.shape[0]
        out.append(sums[lo:lo + n] + sums[lo + n:lo + 2 * n])
        lo += 2 * n
    return out


def _segsum(x):
    return _segsums([x])[0]


def _expand(cols, ja, jb):
    left = _lane((cols.shape[0], PAIR)) < HEAD_DIM
    return jnp.where(left, cols[:, ja:ja + 1], cols[:, jb:jb + 1])


def _chunk_scan(x, combine, identity):
    pos = _row(x.shape) & (CHUNK - 1)
    step = 1
    while step < CHUNK:
        x = combine(x, jnp.where(pos >= step, pltpu.roll(x, step, axis=0), identity))
        step *= 2
    return x


def _causal_conv(z, carry_ref, w_ref):
    rows = z.shape[0]
    prev = carry_ref[...]
    carry_ref[...] = z[rows - CARRY_ROWS:, :]
    head_rows = _row(prev.shape)
    y = z * w_ref[CONV_WIDTH - 1:CONV_WIDTH, :]
    for k in range(1, CONV_WIDTH):
        zr = pltpu.roll(z, k, axis=0)
        pr = pltpu.roll(prev, k, axis=0)
        head = jnp.where(head_rows < k, pr, zr[:CARRY_ROWS])
        shifted = jnp.concatenate([head, zr[CARRY_ROWS:]], axis=0)
        y = y + shifted * w_ref[CONV_WIDTH - 1 - k:CONV_WIDTH - k, :]
    return y


def _modulated(x_ref, mod_ref):
    return (x_ref[0] * (1.0 + mod_ref[0, 1:2, :]) + mod_ref[0, 0:1, :]).astype(BF16)


def _pair_masks():
    row = _row((CHUNK, PAIR))
    col = _lane((CHUNK, PAIR)) & (HEAD_DIM - 1)
    return col <= row, col < row, col == row


def _row_form(col2, eye2):
    return jnp.sum(jnp.where(eye2, col2, 0.0), axis=0, keepdims=True)


def _ada_kernel(c_ref, w_ref, b_ref, o_ref):
    o_ref[0] = _dot(_silu(c_ref[...]), w_ref[0]) + b_ref[0]


def _ada_mod(c, ada_w, ada_b):
    depth, d, n = ada_w.shape
    rows = c.shape[0]
    b = -(-rows // F32_SUBLANES) * F32_SUBLANES
    c = jnp.pad(c, ((0, b - rows), (0, 0)))
    return pl.pallas_call(
        _ada_kernel,
        grid=(depth, n // ADA_COLS),
        in_specs=[
            pl.BlockSpec((b, d), lambda l, j: (0, 0)),
            pl.BlockSpec((1, d, ADA_COLS), lambda l, j: (l, 0, j)),
            pl.BlockSpec((1, 1, ADA_COLS), lambda l, j: (l, 0, j)),
        ],
        out_specs=pl.BlockSpec((1, b, ADA_COLS), lambda l, j: (l, 0, j)),
        out_shape=jax.ShapeDtypeStruct((depth, b, n), F32),
        compiler_params=pltpu.CompilerParams(vmem_limit_bytes=VMEM_LIMIT_BYTES),
        name="ada_mod",
    )(c, ada_w, ada_b.reshape(depth, 1, n))[:, :rows]


def _swiglu_sublayer(x, mod_ref, w13_ref, w2_ref, g_ref, b_ref):
    h = (x * (1.0 + mod_ref[0, 1:2, :]) + mod_ref[0, 0:1, :]).astype(BF16)
    acc = jnp.zeros(x.shape, F32)
    for c in range(D_FF // FFN_COLS):
        lo = c * FFN_COLS
        a = jnp.dot(h, w13_ref[:, lo:lo + FFN_COLS], preferred_element_type=F32)
        b = jnp.dot(h, w13_ref[:, D_FF + lo:D_FF + lo + FFN_COLS], preferred_element_type=F32)
        act = (_silu(a) * b).astype(BF16)
        acc = acc + jnp.dot(act, w2_ref[lo:lo + FFN_COLS, :], preferred_element_type=F32)
    y = ALPHA * x + (0.5 * (1.0 + mod_ref[0, 2:3, :])) * acc
    return _layer_norm(y, g_ref[...], b_ref[...])


def _ffn_kernel(x_ref, mod_ref, w13_ref, w2_ref, g_ref, b_ref, o_ref):
    o_ref[0] = _swiglu_sublayer(x_ref[0], mod_ref, w13_ref, w2_ref, g_ref, b_ref)


def _resident(stack, *lead):
    block = (None,) * len(lead) + tuple(stack.shape[len(lead):])
    return pl.BlockSpec(block, lambda i, j: (*lead, 0, 0), pipeline_mode=pl.Buffered(1))


def _ffn(x, mod, w13, w2, where, ln_g, ln_b):
    b, t, d = x.shape
    const = lambda i, j: (0, 0)
    return pl.pallas_call(
        _ffn_kernel,
        grid=(b, t // FFN_ROWS),
        in_specs=[
            pl.BlockSpec((1, FFN_ROWS, d), lambda i, j: (i, j, 0)),
            pl.BlockSpec((1, 3, d), lambda i, j: (i, 0, 0)),
            _resident(w13, *where),
            _resident(w2, *where),
            pl.BlockSpec((1, d), const),
            pl.BlockSpec((1, d), const),
        ],
        out_specs=pl.BlockSpec((1, FFN_ROWS, d), lambda i, j: (i, j, 0)),
        out_shape=jax.ShapeDtypeStruct(x.shape, F32),
        compiler_params=pltpu.CompilerParams(vmem_limit_bytes=VMEM_LIMIT_BYTES),
        name="ffn",
    )(x, mod, w13, w2, ln_g.reshape(1, d), ln_b.reshape(1, d))


GDN_QKV = 3 * W_GDN


def _gdn_kernel(x_ref, mod_ref, w_ref, conv_ref, par_ref, norm_ref, o_ref, carry_ref, s_ref):
    @pl.when(pl.program_id(1) == 0)
    def _():
        carry_ref[...] = jnp.zeros_like(carry_ref)
        s_ref[...] = jnp.zeros_like(s_ref)

    proj = jnp.dot(_modulated(x_ref, mod_ref), w_ref[...], preferred_element_type=F32)
    qkv = _silu(_causal_conv(proj[:, :GDN_QKV], carry_ref, conv_ref))
    zs = proj[:, GDN_QKV + W_GDN:]
    g = -jnp.exp(par_ref[0:1, :]) * _softplus(zs + par_ref[1:2, :])
    beta = _sigmoid(zs)
    gcum = _chunk_scan(g, jnp.add, 0.0)
    incl2, strict2, eye2 = _pair_masks()
    eye_f = jnp.where(eye2, 1.0, 0.0)
    pairs = range(H_GDN // 2)
    chunks = range(MIX_ROWS // CHUNK)

    q_n, k_n = [], []
    for p in pairs:
        q_all = qkv[:, p * PAIR:(p + 1) * PAIR]
        k_all = qkv[:, W_GDN + p * PAIR:W_GDN + (p + 1) * PAIR]
        q_n.append(q_all * lax.rsqrt(_segsum(q_all * q_all) + NORM_EPS) * (HEAD_DIM ** -0.5))
        k_n.append(k_all * lax.rsqrt(_segsum(k_all * k_all) + NORM_EPS))
    st = {}

    def prepare(p, c):
        rows = slice(c * CHUNK, (c + 1) * CHUNK)
        q2, k2 = q_n[p][rows], k_n[p][rows]
        v2 = qkv[rows, 2 * W_GDN + p * PAIR:2 * W_GDN + (p + 1) * PAIR]
        g2 = _expand(gcum[rows], 2 * p, 2 * p + 1)
        b2 = _expand(beta[rows], H_GDN + 2 * p, H_GDN + 2 * p + 1)
        g_last = g2[CHUNK - 1:CHUNK, :]
        decay = jnp.exp(jnp.where(incl2, g2 - _row_form(g2, eye2), -jnp.inf))
        exp_g = jnp.exp(g2)
        kb = k2 * b2
        gram = _dot_nt(jnp.concatenate([kb, q2], axis=0), _bd(k2))
        st[p, c] = dict(
            xp=-jnp.where(strict2, gram[:CHUNK] * decay, 0.0),
            attn=jnp.where(incl2, gram[CHUNK:] * decay, 0.0),
            rhs=jnp.concatenate([_bd(v2 * b2), _bd(kb * exp_g)], axis=1),
            q_dec=q2 * exp_g,
            k_dec_t=_bd(k2 * jnp.exp(g_last - g2), F32).T,
            g_end=jnp.exp(g_last))

    def inverse_rounds(group):
        tinv = {u: eye_f + st[u]["xp"] for u in group}

        def square():
            for u in group:
                st[u]["xp"] = _pmm(st[u]["xp"], st[u]["xp"])

        def fold_and_square():
            for u in group:
                x = st[u]["xp"]
                both = _pmm(jnp.concatenate([tinv[u], x], axis=0), x)
                tinv[u] = tinv[u] + both[:CHUNK]
                st[u]["xp"] = both[CHUNK:]

        def fold():
            for u in group:
                tinv[u] = tinv[u] + _pmm(tinv[u], st[u]["xp"])

        return tinv, [square] + [fold_and_square] * 4 + [fold]

    def solve(group, tinv):
        for u in group:
            uw = _dot(tinv[u], st[u]["rhs"])
            st[u]["u"], st[u]["w"] = uw[:, :PAIR], uw[:, PAIR:]

    def mean_square(c):
        sums = _segsums([st[p, c]["o"] * st[p, c]["o"] for p in pairs])
        for p in pairs:
            st[p, c]["ms"] = sums[p] * (1.0 / HEAD_DIM)

    def emit(c):
        rows = slice(c * CHUNK, (c + 1) * CHUNK)
        for p in pairs:
            e = st[p, c]
            z2 = proj[rows, GDN_QKV + p * PAIR:GDN_QKV + (p + 1) * PAIR]
            o_ref[0, rows, p * PAIR:(p + 1) * PAIR] = (
                e["o"] * lax.rsqrt(e["ms"] + NORM_EPS) * norm_ref[...] * _silu(z2)).astype(o_ref.dtype)

    states = [s_ref[p] for p in pairs]

    def recur(c):
        ws_qs = [_dot(jnp.concatenate([st[p, c]["w"], st[p, c]["q_dec"]], axis=0), states[p])
                 for p in pairs]
        if c > 0:
            mean_square(c - 1)
        for p in pairs:
            e = st[p, c]
            v_new = e["u"] - ws_qs[p][:CHUNK]
            both = _dot(jnp.concatenate([e["attn"], e["k_dec_t"]], axis=0), _bd(v_new))
            e["o"] = ws_qs[p][CHUNK:] + both[:CHUNK]
            states[p] = states[p] * e["g_end"] + both[CHUNK:]
        if c > 0:
            emit(c - 1)

    half = len(chunks) // 2
    first = [(p, c) for c in chunks[:half] for p in pairs]
    second = [(p, c) for c in chunks[half:] for p in pairs]
    for u in first:
        prepare(*u)
    tinv, rounds = inverse_rounds(first)
    share = -(-len(second) // len(rounds))
    for r, run in enumerate(rounds):
        run()
        for u in second[r * share:(r + 1) * share]:
            prepare(*u)
    solve(first, tinv)
    tinv, rounds = inverse_rounds(second)
    for r, run in enumerate(rounds):
        run()
        if r < half:
            recur(chunks[r])
    solve(second, tinv)
    for c in chunks[min(len(rounds), half):]:
        recur(c)
    for p in pairs:
        s_ref[p] = states[p]
    mean_square(chunks[-1])
    emit(chunks[-1])


def _gdn(x, mod, w, conv, par, norm):
    b, t, d = x.shape
    const = lambda i, j: (0, 0)
    return pl.pallas_call(
        _gdn_kernel,
        grid=(b, t // MIX_ROWS),
        in_specs=[
            pl.BlockSpec((1, MIX_ROWS, d), lambda i, j: (i, j, 0)),
            pl.BlockSpec((1, 3, d), lambda i, j: (i, 0, 0)),
            pl.BlockSpec(w.shape, const),
            pl.BlockSpec(conv.shape, const),
            pl.BlockSpec(par.shape, const),
            pl.BlockSpec(norm.shape, const),
        ],
        out_specs=pl.BlockSpec((1, MIX_ROWS, W_GDN), lambda i, j: (i, j, 0)),
        out_shape=jax.ShapeDtypeStruct((b, t, W_GDN), BF16),
        scratch_shapes=[
            pltpu.VMEM((CARRY_ROWS, GDN_QKV), F32),
            pltpu.VMEM((H_GDN // 2, PAIR, PAIR), F32),
        ],
        compiler_params=pltpu.CompilerParams(vmem_limit_bytes=VMEM_LIMIT_BYTES),
        name="gdn",
    )(x, mod, w, conv, par, norm)


ML_QK = 2 * W_MLSTM


def _mlstm_kernel(x_ref, mod_ref, w_ref, conv_ref, par_ref, norm_ref, o_ref,
                  carry_ref, c_ref, n_ref, m_ref):
    @pl.when(pl.program_id(1) == 0)
    def _():
        carry_ref[...] = jnp.zeros_like(carry_ref)
        c_ref[...] = jnp.zeros_like(c_ref)
        n_ref[...] = jnp.zeros_like(n_ref)
        m_ref[...] = jnp.zeros_like(m_ref)

    proj = jnp.dot(_modulated(x_ref, mod_ref), w_ref[...], preferred_element_type=F32)
    qk = _silu(_causal_conv(proj[:, :ML_QK], carry_ref, conv_ref))
    zs = proj[:, ML_QK + 2 * W_MLSTM:] + par_ref[0:1, :]
    i_pre = zs
    logf = -_softplus(-zs)
    bcum = _chunk_scan(logf, jnp.add, 0.0)
    bcum_i = pltpu.roll(bcum, SMALL - H_MLSTM, axis=1)
    i_minus_b = i_pre - bcum_i
    run_max = _chunk_scan(i_minus_b, jnp.maximum, -jnp.inf)
    incl2, _, eye2 = _pair_masks()
    pairs = range(H_MLSTM // 2)
    chunks = range(MIX_ROWS // CHUNK)
    units = [(p, c) for p in pairs for c in chunks]

    def operands(p, c):
        rows = slice(c * CHUNK, (c + 1) * CHUNK)
        q2 = qk[rows, p * PAIR:(p + 1) * PAIR]
        k2 = qk[rows, W_MLSTM + p * PAIR:W_MLSTM + (p + 1) * PAIR] * (HEAD_DIM ** -0.5)
        v2 = proj[rows, ML_QK + p * PAIR:ML_QK + (p + 1) * PAIR]
        return q2, k2, v2

    st = {}
    for p in pairs:
        m_state = m_ref[p]
        for c in chunks:
            rows = slice(c * CHUNK, (c + 1) * CHUNK)
            b2 = _expand(bcum_i[rows], 2 * p, 2 * p + 1)
            i2 = _expand(i_pre[rows], 2 * p, 2 * p + 1)
            d_max = b2 + _expand(run_max[rows], 2 * p, 2 * p + 1)
            b_last = b2[CHUNK - 1:CHUNK, :]
            w_end = b_last - b2 + i2
            inter = b2 + m_state
            m_t = jnp.maximum(inter, d_max)
            inter_end = b_last + m_state
            m_new = jnp.maximum(inter_end, jnp.max(w_end, axis=0, keepdims=True))
            d2 = jnp.where(incl2, b2 + _row_form(i2 - b2, eye2), -jnp.inf)
            st[p, c] = dict(m_t=m_t, a_inter=jnp.exp(inter - m_t), decay=jnp.exp(d2 - m_t),
                            s=jnp.exp(w_end - m_new), dec=jnp.exp(inter_end - m_new))
            m_state = m_new
        m_ref[p] = m_state

    for p, c in units:
        q2, k2, v2 = operands(p, c)
        e = st[p, c]
        e["p2"] = _pmm_nt(q2, k2) * e["decay"]
        sk = e["s"] * k2
        e["sk_t"] = _bd(sk, F32).T
        e["ksum"] = jnp.sum(sk, axis=0, keepdims=True)
    for p, c in units:
        e = st[p, c]
        both = _dot(jnp.concatenate([e["p2"], e["sk_t"]], axis=0), _bd(operands(p, c)[2]))
        e["intra"], e["kv"] = both[:CHUNK], both[CHUNK:]

    for p in pairs:
        c_state, n_state = c_ref[p], n_ref[p]
        for c in chunks:
            e = st[p, c]
            e["c_in"], e["n_in"] = c_state, n_state
            c_state = e["dec"] * c_state + e["kv"]
            n_state = e["dec"] * n_state + e["ksum"]
        c_ref[p], n_ref[p] = c_state, n_state

    for p, c in units:
        q2 = operands(p, c)[0]
        e = st[p, c]
        e["num"] = e["a_inter"] * _dot(q2, e["c_in"]) + e["intra"]
        e["den_terms"] = e["a_inter"] * q2 * e["n_in"] + e["p2"]
    dens = _segsums([st[u]["den_terms"] for u in units])
    hids = []
    for (p, c), den in zip(units, dens):
        rows = slice(c * CHUNK, (c + 1) * CHUNK)
        e = st[p, c]
        hid = e["num"] / jnp.maximum(jnp.abs(den), jnp.exp(-e["m_t"]))
        hids.append(hid * _sigmoid(proj[rows, ML_QK + W_MLSTM + p * PAIR:ML_QK + W_MLSTM + (p + 1) * PAIR]))
    for u, hid, total in zip(units, hids, _segsums(hids)):
        st[u]["cen"] = hid - total * (1.0 / HEAD_DIM)
    for u, total in zip(units, _segsums([st[u]["cen"] * st[u]["cen"] for u in units])):
        st[u]["var"] = total * (1.0 / HEAD_DIM)
    for p, c in units:
        rows = slice(c * CHUNK, (c + 1) * CHUNK)
        lanes = slice(p * PAIR, (p + 1) * PAIR)
        e = st[p, c]
        o_ref[0, rows, lanes] = (e["cen"] * lax.rsqrt(e["var"] + NORM_EPS)
                                 * norm_ref[:, lanes]).astype(o_ref.dtype)


def _mlstm(x, mod, w, conv, par, norm):
    b, t, d = x.shape
    const = lambda i, j: (0, 0)
    return pl.pallas_call(
        _mlstm_kernel,
        grid=(b, t // MIX_ROWS),
        in_specs=[
            pl.BlockSpec((1, MIX_ROWS, d), lambda i, j: (i, j, 0)),
            pl.BlockSpec((1, 3, d), lambda i, j: (i, 0, 0)),
            pl.BlockSpec(w.shape, const),
            pl.BlockSpec(conv.shape, const),
            pl.BlockSpec(par.shape, const),
            pl.BlockSpec(norm.shape, const),
        ],
        out_specs=pl.BlockSpec((1, MIX_ROWS, W_MLSTM), lambda i, j: (i, j, 0)),
        out_shape=jax.ShapeDtypeStruct((b, t, W_MLSTM), BF16),
        scratch_shapes=[
            pltpu.VMEM((CARRY_ROWS, ML_QK), F32),
            pltpu.VMEM((H_MLSTM // 2, PAIR, PAIR), F32),
            pltpu.VMEM((H_MLSTM // 2, 1, PAIR), F32),
            pltpu.VMEM((H_MLSTM // 2, 1, PAIR), F32),
        ],
        compiler_params=pltpu.CompilerParams(vmem_limit_bytes=VMEM_LIMIT_BYTES),
        name="mlstm",
    )(x, mod, w, conv, par, norm)


def _moba_kernel(x_ref, mod_ref, w_ref, o_ref, k_ref, vt_ref, kmean_ref,
                 s0_ref, s1_ref, e0_ref, e1_ref, acc_ref):
    t = pl.program_id(1)
    n_blk = kmean_ref.shape[0]

    @pl.when(t == 0)
    def _():
        kmean_ref[...] = jnp.zeros_like(kmean_ref)

    proj = jnp.dot(_modulated(x_ref, mod_ref), w_ref[...], preferred_element_type=F32)
    q = proj[:, :W_MOBA]
    k = proj[:, W_MOBA:2 * W_MOBA]
    v = proj[:, 2 * W_MOBA:]
    k_ref[t] = k.astype(BF16)
    ones_rows = jnp.where(_row((VT_ROWS - PAIR, MOBA_BLOCK)) == 0, 1.0, 0.0)
    for p in range(H_MOBA // 2):
        vt_ref[t, p * VT_ROWS:(p + 1) * VT_ROWS, :] = jnp.concatenate(
            [v[:, p * PAIR:(p + 1) * PAIR].T, ones_rows], axis=0).astype(BF16)
    k_mean = jnp.mean(k, axis=0, keepdims=True)
    kmean_ref[...] = jnp.where(_row(kmean_ref.shape) == t, k_mean, kmean_ref[...])
    kmean = kmean_ref[...]

    n_q = 2 * MOBA_BLOCK
    blk = _row((n_blk, n_q)).astype(F32)
    t_f = t.astype(F32)
    causal = _row((MOBA_BLOCK, n_q)) <= (_lane((MOBA_BLOCK, n_q)) & (MOBA_BLOCK - 1))
    left = _lane((MOBA_BLOCK, PAIR)) < HEAD_DIM

    pairs = range(H_MOBA // 2)
    s_bufs, e_bufs = (s0_ref, s1_ref), (e0_ref, e1_ref)
    lanes = [slice(p * PAIR, (p + 1) * PAIR) for p in pairs]
    q_bd = [_bd(q[:, lanes[p]], F32) for p in pairs]
    gates = []
    for p in pairs:
        q_hi, q_lo = _split(q_bd[p])
        m_hi, m_lo = _split(kmean[:, lanes[p]])
        gate = _dot_nt(m_hi, q_hi) + _dot_nt(m_hi, q_lo) + _dot_nt(m_lo, q_hi)
        gates.append(jnp.where(blk < t_f, gate, -jnp.inf))
    qs_all = [(q_bd[p] * (HEAD_DIM ** -0.5 * LOG2_E)).astype(BF16) for p in pairs]
    own = [jnp.where(causal, _dot_nt(k[:, lanes[p]], qs_all[p]), -jnp.inf) for p in pairs]
    picks_all = []
    for p in pairs:
        gate, picks = gates[p], []
        for r in range(MOBA_TOPK):
            best = jnp.max(gate, axis=0, keepdims=True)
            idx = jnp.min(jnp.where(gate == best, blk, float(n_blk)), axis=0, keepdims=True)
            gate = jnp.where(blk == idx, -jnp.inf, gate)
            picks.append(jnp.where(r < t, idx, -1.0))
        picks_all.append(picks)
    init = []
    for p in pairs:
        m_run = jnp.max(own[p], axis=0, keepdims=True)
        e = jnp.exp2(own[p] - m_run)
        acc_ref[p] = _dot(vt_ref[t, p * VT_ROWS:(p + 1) * VT_ROWS, :], e)
        s_bufs[0][p] = _dot_nt(k_ref[0, :, lanes[p]], qs_all[p])
        e_bufs[1][p] = jnp.zeros((MOBA_BLOCK, n_q), BF16)
        init.append((m_run, jnp.ones((1, n_q), F32)))

    def step(i, states, cur, nxt):
        i_f = i.astype(F32)
        for p in pairs:
            vt_rows = slice(p * VT_ROWS, (p + 1) * VT_ROWS)
            acc_ref[p] = acc_ref[p] * states[p][1] + _dot(
                vt_ref[jnp.maximum(i - 1, 0), vt_rows, :], e_bufs[nxt][p])
        new, shifts = [], []
        for p in pairs:
            m_run = states[p][0]
            chosen = (picks_all[p][0] == i_f) | (picks_all[p][1] == i_f) | (picks_all[p][2] == i_f)
            bias = jnp.where(chosen, 0.0, -jnp.inf)
            m_new = jnp.maximum(m_run, jnp.max(s_bufs[cur][p], axis=0, keepdims=True) + bias)
            shifts.append(bias - m_new)
            new.append((m_new, jnp.exp2(m_run - m_new)))
        for p in pairs:
            e_bufs[cur][p] = jnp.exp2(s_bufs[cur][p] + shifts[p]).astype(BF16)
        for p in pairs:
            rows = slice(p * PAIR, (p + 1) * PAIR)
            s_bufs[nxt][p] = _dot_nt(k_ref[jnp.minimum(i + 1, t), :, rows], qs_all[p])
        return tuple(new)

    def body(i2, carry):
        return step(2 * i2 + 1, step(2 * i2, carry, 0, 1), 1, 0)

    trips = (t + 1) // 2
    final = lax.fori_loop(0, trips, body, tuple(init))
    last = jnp.minimum(jnp.maximum(2 * trips - 1, 0), t)
    for p in pairs:
        _, scale_prev = final[p]
        vt_rows = slice(p * VT_ROWS, (p + 1) * VT_ROWS)
        acc = acc_ref[p] * scale_prev + _dot(vt_ref[last, vt_rows, :], e_bufs[1][p])
        out = (acc[:PAIR] / acc[PAIR:PAIR + 1]).T
        o_ref[0, :, p * PAIR:(p + 1) * PAIR] = jnp.where(
            left, out[:MOBA_BLOCK], out[MOBA_BLOCK:]).astype(o_ref.dtype)


def _moba(x, mod, w):
    b, t, d = x.shape
    n_blk = t // MOBA_BLOCK
    assert n_blk >= MOBA_TOPK
    const = lambda i, j: (0, 0)
    return pl.pallas_call(
        _moba_kernel,
        grid=(b, n_blk),
        in_specs=[
            pl.BlockSpec((1, MOBA_BLOCK, d), lambda i, j: (i, j, 0)),
            pl.BlockSpec((1, 3, d), lambda i, j: (i, 0, 0)),
            pl.BlockSpec(w.shape, const),
        ],
        out_specs=pl.BlockSpec((1, MOBA_BLOCK, W_MOBA), lambda i, j: (i, j, 0)),
        out_shape=jax.ShapeDtypeStruct((b, t, W_MOBA), BF16),
        scratch_shapes=[
            pltpu.VMEM((n_blk, MOBA_BLOCK, W_MOBA), BF16),
            pltpu.VMEM((n_blk, (H_MOBA // 2) * VT_ROWS, MOBA_BLOCK), BF16),
            pltpu.VMEM((n_blk, W_MOBA), F32),
            pltpu.VMEM((H_MOBA // 2, MOBA_BLOCK, 2 * MOBA_BLOCK), F32),
            pltpu.VMEM((H_MOBA // 2, MOBA_BLOCK, 2 * MOBA_BLOCK), F32),
            pltpu.VMEM((H_MOBA // 2, MOBA_BLOCK, 2 * MOBA_BLOCK), BF16),
            pltpu.VMEM((H_MOBA // 2, MOBA_BLOCK, 2 * MOBA_BLOCK), BF16),
            pltpu.VMEM((H_MOBA // 2, VT_ROWS, 2 * MOBA_BLOCK), F32),
        ],
        compiler_params=pltpu.CompilerParams(vmem_limit_bytes=VMEM_LIMIT_BYTES),
        name="moba",
    )(x, mod, w)


def _mix_ffn_kernel(x_ref, mix_mod_ref, oa_ref, ob_ref, oc_ref, wo_ref, mix_g_ref, mix_b_ref,
                    mod_ref, w13_ref, w2_ref, g_ref, b_ref, o_ref):
    y = jnp.dot(oa_ref[0], wo_ref[:W_GDN, :], preferred_element_type=F32)
    y = y + jnp.dot(ob_ref[0], wo_ref[W_GDN:W_GDN + W_MOBA, :], preferred_element_type=F32)
    y = y + jnp.dot(oc_ref[0], wo_ref[W_GDN + W_MOBA:, :], preferred_element_type=F32)
    v = ALPHA * x_ref[0] + (1.0 + mix_mod_ref[0, 2:3, :]) * y
    x_mid = _layer_norm(v, mix_g_ref[...], mix_b_ref[...])
    o_ref[0] = _swiglu_sublayer(x_mid, mod_ref, w13_ref, w2_ref, g_ref, b_ref)


def _mix_ffn(x, mix_mod, oa, ob, oc, w_out, mix_g, mix_b, mod, w13, w2, where, ln_g, ln_b):
    b, t, d = x.shape
    const = lambda i, j: (0, 0)
    tile = lambda width: pl.BlockSpec((1, FFN_ROWS, width), lambda i, j: (i, j, 0))
    per_batch = pl.BlockSpec((1, 3, d), lambda i, j: (i, 0, 0))
    row = pl.BlockSpec((1, d), const)
    return pl.pallas_call(
        _mix_ffn_kernel,
        grid=(b, t // FFN_ROWS),
        in_specs=[
            tile(d), per_batch, tile(W_GDN), tile(W_MOBA), tile(W_MLSTM),
            _resident(w_out, where[0]), row, row,
            per_batch,
            _resident(w13, *where),
            _resident(w2, *where), row, row,
        ],
        out_specs=tile(d),
        out_shape=jax.ShapeDtypeStruct(x.shape, F32),
        compiler_params=pltpu.CompilerParams(vmem_limit_bytes=VMEM_LIMIT_BYTES),
        name="mix_ffn",
    )(x, mix_mod, oa, ob, oc, w_out, mix_g.reshape(1, d), mix_b.reshape(1, d),
      mod, w13, w2, ln_g.reshape(1, d), ln_b.reshape(1, d))


def _small_cols(*cols):
    cat = jnp.concatenate(cols, axis=-1)
    return jnp.pad(cat, [(0, 0)] * (cat.ndim - 1) + [(0, SMALL - cat.shape[-1])])


def _split_w_in(w_in):
    sizes = [3 * W_GDN, H_GDN, H_GDN, W_GDN, 3 * W_MOBA,
             2 * W_MLSTM, W_MLSTM, H_MLSTM, H_MLSTM, W_MLSTM]
    parts, lo = [], 0
    for s in sizes:
        parts.append(w_in[:, lo:lo + s])
        lo += s
    qkv_a, a_a, b_a, z_a, qkv_b, qk_c, v_c, i_c, f_c, o_c = parts
    w_gdn = jnp.concatenate([qkv_a, z_a, _small_cols(a_a, b_a)], axis=1)
    w_ml = jnp.concatenate([qk_c, v_c, o_c, _small_cols(i_c, f_c)], axis=1)
    return w_gdn, qkv_b, w_ml


def kernel(x, c, ada_w, ada_b, ffn_w13, ffn_w2, w_in, w_out, gdn_conv, gdn_a_log, gdn_dt_bias, gdn_norm,
           mlstm_conv, mlstm_i_bias, mlstm_f_bias, mlstm_norm, ln_g, ln_b):
    b = x.shape[0]
    depth = ada_w.shape[0]
    mod_all = _ada_mod(c, ada_w, ada_b).reshape(depth, b, N_SUB, 3, D_MODEL)
    w_in_bf16, w_out_bf16 = w_in.astype(BF16), w_out.astype(BF16)
    w13_bf16, w2_bf16 = ffn_w13.astype(BF16), ffn_w2.astype(BF16)
    for l in range(depth):
        mod = mod_all[l]
        w_gdn, w_moba, w_ml = _split_w_in(w_in_bf16[l])
        gdn_par = jnp.stack([_small_cols(gdn_a_log[l]), _small_cols(gdn_dt_bias[l])])
        gdn_par = jnp.pad(gdn_par, ((0, 6), (0, 0)))
        ml_par = jnp.pad(_small_cols(mlstm_i_bias[l], mlstm_f_bias[l])[None, :], ((0, 7), (0, 0)))
        gdn_norm2 = jnp.tile(gdn_norm[l], 2).reshape(1, PAIR)

        x = _ffn(x, mod[:, 0], w13_bf16, w2_bf16, (l, 0), ln_g[l, 0], ln_b[l, 0])
        oa = _gdn(x, mod[:, 1], w_gdn, gdn_conv[l], gdn_par, gdn_norm2)
        ob = _moba(x, mod[:, 1], w_moba)
        oc = _mlstm(x, mod[:, 1], w_ml, mlstm_conv[l], ml_par, mlstm_norm[l].reshape(1, W_MLSTM))
        x = _mix_ffn(x, mod[:, 1], oa, ob, oc, w_out_bf16, ln_g[l, 1], ln_b[l, 1],
                     mod[:, 2], w13_bf16, w2_bf16, (l, 1), ln_g[l, 2], ln_b[l, 2])
    return x
```
